```python
import math
import jax, jax.numpy as jnp
from jax import lax
import numpy as np

D_MODEL = 2048
BATCH = 4
SEQ = 4096
DEPTH = 2

D_CONV = 512
N_CONV_GROUPS = 4
D_ATT = 1024
N_HEADS = 8
HEAD_DIM = 128
D_CFM = 512
N_CFM_GROUPS = 4
D_MIX = D_CONV + D_ATT + D_CFM
D_IN = 3 * D_CONV + 3 * D_ATT + 2 * D_CFM
SHORT_CONV_W = 3
CFM_CONV_W = 31
MOBA_BLOCK = 256
MOBA_TOPK = 3
Q_CHUNK = 64
N_GROUPS = 4
EXPERTS_PER_GROUP = 8
N_EXPERTS = N_GROUPS * EXPERTS_PER_GROUP
TOPK_IN_GROUP = 2
D_EXPERT = 512
ROW_BLOCK = 128
EPS = 1e-6
NEG = -1e30

kernel_name = 'hymba_parallel_conv_moba_conformer_hmoe'


def rmsnorm(x, g):
    x32 = x.astype(jnp.float32)
    y = x32 * lax.rsqrt(jnp.mean(x32 * x32, axis=-1, keepdims=True) + EPS)
    return (y * g.astype(jnp.float32)).astype(x.dtype)


def causal_depthwise_conv(x, w):
    width, ch = w.shape
    return lax.conv_general_dilated(
        x, w[:, None, :].astype(x.dtype), window_strides=(1,),
        padding=[(width - 1, 0)], dimension_numbers=('NWC', 'WIO', 'NWC'),
        feature_group_count=ch)


def alibi_slopes(n):
    return jnp.exp2(-8.0 * jnp.arange(1, n + 1, dtype=jnp.float32) / n)


def short_conv_mixer(u, conv_w):
    bg, cg, xv = jnp.split(u, 3, axis=-1)
    return bg * causal_depthwise_conv(cg * xv, conv_w)


def conformer_conv_mixer(u, conv_w, conv_b, ln_g, ln_b):
    a, gate = jnp.split(u, 2, axis=-1)
    h = a * jax.nn.sigmoid(gate)
    h = causal_depthwise_conv(h, conv_w) + conv_b.astype(h.dtype)
    h32 = h.astype(jnp.float32)
    mu = jnp.mean(h32, axis=-1, keepdims=True)
    var = jnp.mean(jnp.square(h32 - mu), axis=-1, keepdims=True)
    hn = (h32 - mu) * lax.rsqrt(var + EPS) * ln_g.astype(jnp.float32) + ln_b.astype(jnp.float32)
    return jax.nn.silu(hn).astype(u.dtype)


def moba_attention(q, k, v):
    bn, s, _ = q.shape
    nb = -(-s // MOBA_BLOCK)
    s_pad = nb * MOBA_BLOCK
    k_top = min(MOBA_TOPK, nb)
    n_chunks = s // Q_CHUNK

    def heads(t):
        return t.reshape(bn, s, N_HEADS, HEAD_DIM).transpose(0, 2, 1, 3)

    pad = ((0, 0), (0, 0), (0, s_pad - s), (0, 0))
    qh = heads(q) * (HEAD_DIM ** -0.5)
    kh = jnp.pad(heads(k), pad)
    vh = jnp.pad(heads(v), pad)
    kb = kh.reshape(bn, N_HEADS, nb, MOBA_BLOCK, HEAD_DIM)
    vb = vh.reshape(bn, N_HEADS, nb, MOBA_BLOCK, HEAD_DIM)
    kmean = jnp.mean(kb.astype(jnp.float32), axis=3)

    qblk = jnp.arange(s) // MOBA_BLOCK
    gate = jnp.einsum('bhsd,bhnd->bhsn', qh.astype(jnp.float32), kmean)
    fully_past = jnp.arange(nb)[None, :] < qblk[:, None]
    gate = jnp.where(fully_past, gate, NEG)
    _, sel = lax.top_k(gate, k_top)
    slopes = alibi_slopes(N_HEADS)

    def chunk(bc):
        b = bc // n_chunks
        t0 = (bc % n_chunks) * Q_CHUNK
        blk = t0 // MOBA_BLOCK
        qc = lax.dynamic_slice_in_dim(lax.dynamic_index_in_dim(qh, b, 0, False), t0, Q_CHUNK, 1)
        selc = lax.dynamic_slice_in_dim(lax.dynamic_index_in_dim(sel, b, 0, False), t0, Q_CHUNK, 1)
        kb_b = lax.dynamic_index_in_dim(kb, b, 0, False)
        vb_b = lax.dynamic_index_in_dim(vb, b, 0, False)
        gather = jax.vmap(lambda xh, ih: xh[ih])
        ks = gather(kb_b, selc)
        vs = gather(vb_b, selc)
        tpos = t0 + jnp.arange(Q_CHUNK)
        kpos = selc[..., None] * MOBA_BLOCK + jnp.arange(MOBA_BLOCK)
        valid = jnp.arange(k_top) < blk
        s_sel = jnp.einsum('hqd,hqrsd->hqrs', qc, ks).astype(jnp.float32)
        s_sel = s_sel - slopes[:, None, None, None] * (tpos[None, :, None, None] - kpos).astype(jnp.float32)
        s_sel = jnp.where(valid[None, None, :, None], s_sel, NEG)
        ko = lax.dynamic_slice_in_dim(lax.dynamic_index_in_dim(kh, b, 0, False), blk * MOBA_BLOCK, MOBA_BLOCK, 1)
        vo = lax.dynamic_slice_in_dim(lax.dynamic_index_in_dim(vh, b, 0, False), blk * MOBA_BLOCK, MOBA_BLOCK, 1)
        kpos_own = blk * MOBA_BLOCK + jnp.arange(MOBA_BLOCK)
        s_own = jnp.einsum('hqd,hsd->hqs', qc, ko).astype(jnp.float32)
        s_own = s_own - slopes[:, None, None] * (tpos[:, None] - kpos_own[None, :]).astype(jnp.float32)
        s_own = jnp.where(kpos_own[None, :] <= tpos[:, None], s_own, NEG)
        logits = jnp.concatenate([s_sel.reshape(N_HEADS, Q_CHUNK, k_top * MOBA_BLOCK), s_own], axis=-1)
        p = jax.nn.softmax(logits, axis=-1)
        p_sel = p[..., :k_top * MOBA_BLOCK].reshape(N_HEADS, Q_CHUNK, k_top, MOBA_BLOCK).astype(vs.dtype)
        p_own = p[..., k_top * MOBA_BLOCK:].astype(vo.dtype)
        return jnp.einsum('hqrs,hqrsd->hqd', p_sel, vs) + jnp.einsum('hqs,hsd->hqd', p_own, vo)

    o = lax.map(chunk, jnp.arange(bn * n_chunks))
    o = o.reshape(bn, n_chunks, N_HEADS, Q_CHUNK, HEAD_DIM).transpose(0, 1, 3, 2, 4)
    return o.reshape(bn, s, D_ATT)


def hierarchical_moe(x, rg_w, rg_b, re_w, re_b, w_gate, w_up, w_down):
    bn, s, d = x.shape
    t = bn * s
    xt = x.reshape(t, d)
    pg = jax.nn.softmax((xt @ rg_w).astype(jnp.float32) + rg_b.astype(jnp.float32), axis=-1)
    grp = jnp.argmax(pg, axis=-1)
    p_grp = jnp.take_along_axis(pg, grp[:, None], axis=1)[:, 0]
    le = ((xt @ re_w).astype(jnp.float32) + re_b.astype(jnp.float32)).reshape(t, N_GROUPS, EXPERTS_PER_GROUP)
    le_g = jnp.take_along_axis(le, grp[:, None, None], axis=1)[:, 0]
    q_top, e_top = lax.top_k(jax.nn.softmax(le_g, axis=-1), TOPK_IN_GROUP)
    gates = p_grp[:, None] * q_top / jnp.sum(q_top, axis=-1, keepdims=True)
    expert = grp[:, None] * EXPERTS_PER_GROUP + e_top

    n_assign = t * TOPK_IN_GROUP
    e_flat = expert.reshape(-1)
    g_flat = gates.reshape(-1)
    tok_flat = jnp.repeat(jnp.arange(t, dtype=jnp.int32), TOPK_IN_GROUP)
    order = jnp.argsort(e_flat)
    e_s, tok_s, g_s = e_flat[order], tok_flat[order], g_flat[order]
    counts = jnp.bincount(e_flat, length=N_EXPERTS)
    padded = (counts + ROW_BLOCK - 1) // ROW_BLOCK * ROW_BLOCK
    start = jnp.cumsum(counts) - counts
    end_p = jnp.cumsum(padded)
    start_p = end_p - padded
    dest = start_p[e_s] + (jnp.arange(n_assign) - start[e_s])
    n_rows = n_assign + N_EXPERTS * ROW_BLOCK
    n_blocks = n_rows // ROW_BLOCK
    row_tok = jnp.zeros((n_rows,), jnp.int32).at[dest].set(tok_s)
    row_w = jnp.zeros((n_rows,), jnp.float32).at[dest].set(g_s)
    blk_expert = jnp.clip(jnp.searchsorted(end_p, jnp.arange(n_blocks) * ROW_BLOCK, side='right'), 0, N_EXPERTS - 1)

    def expert_block(args):
        tok, e = args
        xb = xt[tok]
        h = jax.nn.silu(xb @ w_gate[e]) * (xb @ w_up[e])
        return h @ w_down[e]

    y = lax.map(expert_block, (row_tok.reshape(n_blocks, ROW_BLOCK), blk_expert))
    y = y.reshape(n_rows, d) * row_w[:, None].astype(y.dtype)
    out = jax.ops.segment_sum(y, row_tok, num_segments=t)
    return out.reshape(bn, s, d).astype(x.dtype)


def setup_inputs(seed: int = 0) -> dict:
    key = jax.random.key(seed)
    ks = jax.random.split(key, 20)
    L, D = DEPTH, D_MODEL

    def nrm(k, shape, scale):
        return jax.random.normal(k, shape, jnp.float32) * scale

    return {
        'x': nrm(ks[0], (BATCH, SEQ, D), 1.0),
        'norm1_g': 1.0 + nrm(ks[1], (L, D), 0.01),
        'w_in': nrm(ks[2], (L, D, D_IN), D ** -0.5),
        'conv_a_w': nrm(ks[3], (L, SHORT_CONV_W, D_CONV), SHORT_CONV_W ** -0.5),
        'conv_c_w': nrm(ks[4], (L, CFM_CONV_W, D_CFM), CFM_CONV_W ** -0.5),
        'conv_c_b': nrm(ks[5], (L, D_CFM), 0.01),
        'ln_c_g': 1.0 + nrm(ks[6], (L, D_CFM), 0.01),
        'ln_c_b': nrm(ks[7], (L, D_CFM), 0.01),
        'w_out': nrm(ks[8], (L, D_MIX, D), D_MIX ** -0.5),
        'norm2_g': 1.0 + nrm(ks[9], (L, D), 0.01),
        'router_g_w': nrm(ks[10], (L, D, N_GROUPS), D ** -0.5),
        'router_g_b': nrm(ks[11], (L, N_GROUPS), 0.01),
        'router_e_w': nrm(ks[12], (L, D, N_EXPERTS), D ** -0.5),
        'router_e_b': nrm(ks[13], (L, N_EXPERTS), 0.01),
        'w_gate': nrm(ks[14], (L, N_EXPERTS, D, D_EXPERT), D ** -0.5),
        'w_up': nrm(ks[15], (L, N_EXPERTS, D, D_EXPERT), D ** -0.5),
        'w_down': nrm(ks[16], (L, N_EXPERTS, D_EXPERT, D), D_EXPERT ** -0.5),
        'final_g': 1.0 + nrm(ks[17], (D,), 0.01),
    }


def reference(x, norm1_g, w_in, conv_a_w, conv_c_w, conv_c_b, ln_c_g, ln_c_b, w_out,
              norm2_g, router_g_w, router_g_b, router_e_w, router_e_b,
              w_gate, w_up, w_down, final_g):
    splits = [3 * D_CONV, 3 * D_CONV + D_ATT, 3 * D_CONV + 2 * D_ATT,
              3 * D_CONV + 3 * D_ATT]
    for l in range(DEPTH):
        h = rmsnorm(x, norm1_g[l])
        u = h @ w_in[l]
        u_a, u_q, u_k, u_v, u_c = jnp.split(u, splits, axis=-1)
        y_a = short_conv_mixer(u_a, conv_a_w[l])
        y_b = moba_attention(u_q, u_k, u_v)
        y_c = conformer_conv_mixer(u_c, conv_c_w[l], conv_c_b[l], ln_c_g[l], ln_c_b[l])
        x = x + jnp.concatenate([y_a, y_b, y_c], axis=-1) @ w_out[l]
        h = rmsnorm(x, norm2_g[l])
        x = x + hierarchical_moe(h, router_g_w[l], router_g_b[l], router_e_w[l], router_e_b[l],
                                 w_gate[l], w_up[l], w_down[l])
    return rmsnorm(x, final_g)
```

```python
import functools

import jax
import jax.numpy as jnp
from jax import lax
from jax.experimental import pallas as pl
from jax.experimental.pallas import tpu as pltpu

D_MODEL = 2048
D_CONV = 512
D_ATT = 1024
N_HEADS = 8
HEAD_DIM = 128
D_CFM = 512
D_MIX = D_CONV + D_ATT + D_CFM
D_IN = 3 * D_CONV + 3 * D_ATT + 2 * D_CFM
SHORT_CONV_W = 3
CFM_CONV_W = 31
MOBA_BLOCK = 256
MOBA_TOPK = 3
N_GROUPS = 4
EXPERTS_PER_GROUP = 8
N_EXPERTS = N_GROUPS * EXPERTS_PER_GROUP
TOPK_IN_GROUP = 2
D_EXPERT = 512
EPS = 1e-6
NEG = -1e30

LANES = 128
VMEM_LIMIT = 56 * 1024 * 1024

IN_TM = 1024
IN_TN = 512
MIX_TS = 256
HALO_A = 8
HALO_C = 32
OUT_TM = 256
ROUTER_N = LANES
FFN_TM = 256
COMB_TM = 256

BF16 = jnp.bfloat16
F32 = jnp.float32


def _cparams(sem):
    return pltpu.CompilerParams(dimension_semantics=sem, vmem_limit_bytes=VMEM_LIMIT)


def _in_proj_kernel(x_ref, g_ref, w_ref, o_ref, hn_ref):
    @pl.when(pl.program_id(1) == 0)
    def _():
        x = x_ref[...]
        ms = jnp.mean(x * x, axis=-1, keepdims=True)
        hn_ref[...] = (x * lax.rsqrt(ms + EPS) * g_ref[...]).astype(BF16)

    o_ref[...] = jnp.dot(hn_ref[...], w_ref[...], preferred_element_type=F32).astype(o_ref.dtype)


def _in_proj(x, g, w_bf):
    t = x.shape[0]
    return pl.pallas_call(
        _in_proj_kernel,
        grid=(t // IN_TM, D_IN // IN_TN),
        in_specs=[
            pl.BlockSpec((IN_TM, D_MODEL), lambda i, j: (i, 0)),
            pl.BlockSpec((1, D_MODEL), lambda i, j: (0, 0)),
            pl.BlockSpec((D_MODEL, IN_TN), lambda i, j: (0, j)),
        ],
        out_specs=pl.BlockSpec((IN_TM, IN_TN), lambda i, j: (i, j)),
        out_shape=jax.ShapeDtypeStruct((t, D_IN), BF16),
        scratch_shapes=[pltpu.VMEM((IN_TM, D_MODEL), BF16)],
        compiler_params=_cparams(("arbitrary", "arbitrary")),
        name="in_proj",
    )(x, g.reshape(1, D_MODEL), w_bf)


def _mix_kernel(bg_ref, cg_ref, xv_ref, a_ref, gt_ref, wa_ref, wc_ref, bc_ref, lg_ref, lb_ref,
                ya_ref, yc_ref, bufa, bufc):
    s = pl.program_id(1)
    ts = MIX_TS

    @pl.when(s == 0)
    def _():
        bufa[0:HALO_A, :] = jnp.zeros((HALO_A, D_CONV), F32)
        bufc[0:HALO_C, :] = jnp.zeros((HALO_C, D_CFM), F32)

    @pl.when(s > 0)
    def _():
        bufa[0:HALO_A, :] = bufa[ts:ts + HALO_A, :]
        bufc[0:HALO_C, :] = bufc[ts:ts + HALO_C, :]

    bufa[HALO_A:HALO_A + ts, :] = cg_ref[...].astype(F32) * xv_ref[...].astype(F32)
    acc = jnp.zeros((ts, D_CONV), F32)
    for j in range(SHORT_CONV_W):
        off = HALO_A - (SHORT_CONV_W - 1) + j
        acc = acc + wa_ref[j:j + 1, :] * bufa[off:off + ts, :]
    ya_ref[...] = (bg_ref[...].astype(F32) * acc).astype(ya_ref.dtype)

    a = a_ref[...].astype(F32)
    gate = gt_ref[...].astype(F32)
    bufc[HALO_C:HALO_C + ts, :] = a * jax.nn.sigmoid(gate)
    acc = jnp.zeros((ts, D_CFM), F32) + bc_ref[...]
    for j in range(CFM_CONV_W):
        off = HALO_C - (CFM_CONV_W - 1) + j
        acc = acc + wc_ref[j:j + 1, :] * bufc[off:off + ts, :]
    mu = jnp.mean(acc, axis=-1, keepdims=True)
    cen = acc - mu
    var = jnp.mean(cen * cen, axis=-1, keepdims=True)
    hn = cen * lax.rsqrt(var + EPS) * lg_ref[...] + lb_ref[...]
    yc_ref[...] = (hn * jax.nn.sigmoid(hn)).astype(yc_ref.dtype)


def _mixers(u, wa, wc, bc, lg, lb, batch, seq):
    t = u.shape[0]
    nst = seq // MIX_TS
    row = lambda b, s: b * nst + s
    ublk = lambda c: pl.BlockSpec((MIX_TS, D_CONV), lambda b, s, c=c: (row(b, s), c))
    small = lambda r: pl.BlockSpec((r, D_CONV), lambda b, s: (0, 0))
    c0 = (3 * D_CONV + 3 * D_ATT) // D_CFM
    return pl.pallas_call(
        _mix_kernel,
        grid=(batch, nst),
        in_specs=[ublk(0), ublk(1), ublk(2), ublk(c0), ublk(c0 + 1),
                  small(SHORT_CONV_W), small(CFM_CONV_W), small(1), small(1), small(1)],
        out_specs=[pl.BlockSpec((MIX_TS, D_CONV), lambda b, s: (row(b, s), 0)),
                   pl.BlockSpec((MIX_TS, D_CFM), lambda b, s: (row(b, s), 0))],
        out_shape=[jax.ShapeDtypeStruct((t, D_CONV), BF16), jax.ShapeDtypeStruct((t, D_CFM), BF16)],
        scratch_shapes=[pltpu.VMEM((HALO_A + MIX_TS, D_CONV), F32),
                        pltpu.VMEM((HALO_C + MIX_TS, D_CFM), F32)],
        compiler_params=_cparams(("arbitrary", "arbitrary")),
        name="mixers",
    )(u, u, u, u, u, wa, wc, bc.reshape(1, D_CFM), lg.reshape(1, D_CFM), lb.reshape(1, D_CFM))


_NT = (((1,), (1,)), ((), ()))


def _moba_kernel(slopes_ref, q_ref, k_ref, v_ref, o_ref, kmh_ref, kml_ref):
    h = pl.program_id(1)
    qi = pl.program_id(2)
    blk = MOBA_BLOCK
    nb = k_ref.shape[0] // blk
    scale = HEAD_DIM ** -0.5
    slope = slopes_ref[h]

    @pl.when(qi == 0)
    def _():
        kf = k_ref[...].astype(F32).reshape(nb, blk, HEAD_DIM)
        km = jnp.sum(kf, axis=1) * (1.0 / blk)
        hi = km.astype(BF16)
        lo = (km - hi.astype(F32)).astype(BF16)
        kmh_ref[...] = jnp.zeros(kmh_ref.shape, BF16)
        kml_ref[...] = jnp.zeros(kml_ref.shape, BF16)
        kmh_ref[0:nb, :] = hi
        kml_ref[0:nb, :] = lo

    q = q_ref[...]
    gate = (lax.dot_general(q, kmh_ref[...], _NT, preferred_element_type=F32)
            + lax.dot_general(q, kml_ref[...], _NT, preferred_element_type=F32)) * scale

    lane = lax.broadcasted_iota(jnp.int32, (blk, LANES), 1)
    rank = jnp.zeros((blk, LANES), F32)
    for jp in range(nb):
        col = gate[:, jp:jp + 1]
        beats = (col > gate) | ((col == gate) & (lane > jp))
        rank = rank + jnp.where(beats & (jp < qi), 1.0, 0.0)
    sel = jnp.where((lane < qi) & (rank < MOBA_TOPK), 1.0, 0.0)

    r = lax.broadcasted_iota(jnp.int32, (blk, blk), 0)
    c = lax.broadcasted_iota(jnp.int32, (blk, blk), 1)
    alibi = (c - r).astype(F32) * slope

    def scores(j):
        start = pl.multiple_of(j * blk, blk)
        kj = k_ref[pl.ds(start, blk), :]
        vj = v_ref[pl.ds(start, blk), :]
        s = lax.dot_general(q, kj, _NT, preferred_element_type=F32) * scale + alibi
        return s, vj

    s, vj = scores(qi)
    s = jnp.where(c <= r, s, NEG)
    m = jnp.max(s, axis=1, keepdims=True)
    p = jnp.exp(s - m)
    l = jnp.sum(p, axis=1, keepdims=True)
    acc = jnp.dot(p.astype(BF16), vj, preferred_element_type=F32)

    def past(j, carry):
        m, l, acc = carry
        s, vj = scores(j)
        cj = -slope * ((qi - j) * blk).astype(F32)
        picked = jnp.sum(jnp.where(lane == j, sel, 0.0), axis=1, keepdims=True) > 0.5
        s = jnp.where(picked, s, NEG)
        m_new = jnp.maximum(m, jnp.max(s, axis=1, keepdims=True) + cj)
        p = jnp.exp(s - (m_new - cj))
        alpha = jnp.exp(m - m_new)
        l = alpha * l + jnp.sum(p, axis=1, keepdims=True)
        acc = alpha * acc + jnp.dot(p.astype(BF16), vj, preferred_element_type=F32)
        return m_new, l, acc

    m, l, acc = lax.fori_loop(0, qi, past, (m, l, acc))
    o_ref[...] = (acc / l).astype(o_ref.dtype)


def _moba(u, slopes, batch, seq):
    t = u.shape[0]
    nq = seq // MOBA_BLOCK
    qc = (3 * D_CONV) // HEAD_DIM
    kc = qc + N_HEADS
    vc = kc + N_HEADS
    return pl.pallas_call(
        _moba_kernel,
        grid_spec=pltpu.PrefetchScalarGridSpec(
            num_scalar_prefetch=1,
            grid=(batch, N_HEADS, nq),
            in_specs=[
                pl.BlockSpec((MOBA_BLOCK, HEAD_DIM), lambda b, h, i, s: (b * nq + i, qc + h)),
                pl.BlockSpec((seq, HEAD_DIM), lambda b, h, i, s: (b, kc + h)),
                pl.BlockSpec((seq, HEAD_DIM), lambda b, h, i, s: (b, vc + h)),
            ],
            out_specs=pl.BlockSpec((MOBA_BLOCK, HEAD_DIM), lambda b, h, i, s: (b * nq + i, h)),
            scratch_shapes=[pltpu.VMEM((LANES, HEAD_DIM), BF16), pltpu.VMEM((LANES, HEAD_DIM), BF16)],
        ),
        out_shape=jax.ShapeDtypeStruct((t, D_ATT), BF16),
        compiler_params=_cparams(("arbitrary", "arbitrary", "arbitrary")),
        name="moba",
    )(slopes, u, u, u)


def _out_proj_kernel(x_ref, ya_ref, yb_ref, yc_ref, w_ref, g_ref, wr_ref, br_ref, x1_ref, h2_ref, lg_ref):
    acc = x_ref[...]
    acc = acc + jnp.dot(ya_ref[...], w_ref[0:D_CONV, :], preferred_element_type=F32)
    acc = acc + jnp.dot(yb_ref[...], w_ref[D_CONV:D_CONV + D_ATT, :], preferred_element_type=F32)
    acc = acc + jnp.dot(yc_ref[...], w_ref[D_CONV + D_ATT:D_MIX, :], preferred_element_type=F32)
    x1_ref[...] = acc
    ms = jnp.mean(acc * acc, axis=-1, keepdims=True)
    h2 = acc * lax.rsqrt(ms + EPS) * g_ref[...]
    h2_ref[...] = h2
    lg_ref[...] = jnp.dot(h2.astype(BF16), wr_ref[...], preferred_element_type=F32) + br_ref[...]


def _out_proj(x, ya, yb, yc, w_bf, g, wr_bf, br):
    t = x.shape[0]
    rows = lambda n: pl.BlockSpec((OUT_TM, n), lambda i: (i, 0))
    whole = lambda a, b: pl.BlockSpec((a, b), lambda i: (0, 0))
    return pl.pallas_call(
        _out_proj_kernel,
        grid=(t // OUT_TM,),
        in_specs=[rows(D_MODEL), rows(D_CONV), rows(D_ATT), rows(D_CFM),
                  whole(D_MIX, D_MODEL), whole(1, D_MODEL), whole(D_MODEL, ROUTER_N), whole(1, ROUTER_N)],
        out_specs=[rows(D_MODEL), rows(D_MODEL), rows(ROUTER_N)],
        out_shape=[jax.ShapeDtypeStruct((t, D_MODEL), F32), jax.ShapeDtypeStruct((t, D_MODEL), F32),
                   jax.ShapeDtypeStruct((t, ROUTER_N), F32)],
        compiler_params=_cparams(("arbitrary",)),
        name="out_proj",
    )(x, ya, yb, yc, w_bf, g.reshape(1, D_MODEL), wr_bf, br)


def _gather_rows(idx_ref, n, src_hbm, dst, sem):
    def body(r, carry):
        pltpu.make_async_copy(src_hbm.at[pl.ds(idx_ref[0, 0, r], 1)], dst.at[pl.ds(r, 1)], sem).start()
        return carry
    lax.fori_loop(0, n, body, 0, unroll=8)


def _wait_rows(n, src_hbm, dst, sem):
    pltpu.make_async_copy(src_hbm.at[pl.ds(0, n)], dst, sem).wait()


def _ffn_kernel(be_ref, nb_ref, tok_ref, tokn_ref, h2_hbm, wg_ref, wu_ref, wd_ref, rw_ref, y_ref,
                xbuf, sem, wgb, wub, wdb):
    i = pl.program_id(0)
    nb = nb_ref[0]
    slot = lax.rem(i, 2)

    @pl.when(i == 0)
    def _():
        _gather_rows(tok_ref, FFN_TM, h2_hbm, xbuf.at[0], sem.at[0])

    @pl.when(i + 1 < nb)
    def _():
        _gather_rows(tokn_ref, FFN_TM, h2_hbm, xbuf.at[1 - slot], sem.at[1 - slot])

    @pl.when(i < nb)
    def _():
        prev = be_ref[jnp.maximum(i - 1, 0)]

        @pl.when((i == 0) | (be_ref[i] != prev))
        def _():
            wgb[...] = wg_ref[0].astype(BF16)
            wub[...] = wu_ref[0].astype(BF16)
            wdb[...] = wd_ref[0].astype(BF16)

        _wait_rows(FFN_TM, h2_hbm, xbuf.at[slot], sem.at[slot])
        xb = xbuf[slot].astype(BF16)
        g = jnp.dot(xb, wgb[...], preferred_element_type=F32)
        up = jnp.dot(xb, wub[...], preferred_element_type=F32)
        hid = (g * jax.nn.sigmoid(g) * up).astype(BF16)
        y = jnp.dot(hid, wdb[...], preferred_element_type=F32)
        y_ref[...] = y * rw_ref[...]

    @pl.when(i >= nb)
    def _():
        y_ref[...] = jnp.zeros(y_ref.shape, y_ref.dtype)


def _ffn(h2, blk_expert, n_used, row_tok, row_w, w_gate, w_up, w_down):
    n_rows = row_tok.shape[0]
    n_blocks = n_rows // FFN_TM
    tok3 = row_tok.reshape(n_blocks, 1, FFN_TM)
    wspec = lambda a, b: pl.BlockSpec((1, a, b), lambda i, be, nb: (be[i], 0, 0))
    return pl.pallas_call(
        _ffn_kernel,
        grid_spec=pltpu.PrefetchScalarGridSpec(
            num_scalar_prefetch=2,
            grid=(n_blocks,),
            in_specs=[
                pl.BlockSpec((1, 1, FFN_TM), lambda i, be, nb: (i, 0, 0), memory_space=pltpu.SMEM),
                pl.BlockSpec((1, 1, FFN_TM), lambda i, be, nb: (jnp.minimum(i + 1, n_blocks - 1), 0, 0),
                             memory_space=pltpu.SMEM),
                pl.BlockSpec(memory_space=pl.ANY),
                wspec(D_MODEL, D_EXPERT), wspec(D_MODEL, D_EXPERT), wspec(D_EXPERT, D_MODEL),
                pl.BlockSpec((FFN_TM, 1), lambda i, be, nb: (i, 0)),
            ],
            out_specs=pl.BlockSpec((FFN_TM, D_MODEL), lambda i, be, nb: (i, 0)),
            scratch_shapes=[
                pltpu.VMEM((2, FFN_TM, D_MODEL), F32),
                pltpu.SemaphoreType.DMA((2,)),
                pltpu.VMEM((D_MODEL, D_EXPERT), BF16),
                pltpu.VMEM((D_MODEL, D_EXPERT), BF16),
                pltpu.VMEM((D_EXPERT, D_MODEL), BF16),
            ],
        ),
        out_shape=jax.ShapeDtypeStruct((n_rows, D_MODEL), F32),
        compiler_params=_cparams(("arbitrary",)),
        name="ffn",
    )(blk_expert, n_used, tok3, tok3, h2, w_gate, w_up, w_down, row_w.reshape(n_rows, 1))


def _combine_kernel(pos_ref, posn_ref, x1_ref, y_hbm, g_ref, o_ref, ybuf, sem, *, final_norm):
    i = pl.program_id(0)
    n = pl.num_programs(0)
    slot = lax.rem(i, 2)

    @pl.when(i == 0)
    def _():
        _gather_rows(pos_ref, 2 * COMB_TM, y_hbm, ybuf.at[0], sem.at[0])

    @pl.when(i + 1 < n)
    def _():
        _gather_rows(posn_ref, 2 * COMB_TM, y_hbm, ybuf.at[1 - slot], sem.at[1 - slot])

    _wait_rows(2 * COMB_TM, y_hbm, ybuf.at[slot], sem.at[slot])
    out = x1_ref[...] + (ybuf[slot, 0:COMB_TM, :] + ybuf[slot, COMB_TM:2 * COMB_TM, :])
    if final_norm:
        ms = jnp.mean(out * out, axis=-1, keepdims=True)
        out = out * lax.rsqrt(ms + EPS) * g_ref[...]
    o_ref[...] = out


def _combine(x1, y, pos, g, final_norm):
    t = x1.shape[0]
    nt = t // COMB_TM
    pos3 = pos.reshape(nt, COMB_TM, TOPK_IN_GROUP).transpose(0, 2, 1).reshape(nt, 1, 2 * COMB_TM)
    return pl.pallas_call(
        functools.partial(_combine_kernel, final_norm=final_norm),
        grid=(nt,),
        in_specs=[
            pl.BlockSpec((1, 1, 2 * COMB_TM), lambda i: (i, 0, 0), memory_space=pltpu.SMEM),
            pl.BlockSpec((1, 1, 2 * COMB_TM), lambda i: (jnp.minimum(i + 1, nt - 1), 0, 0),
                         memory_space=pltpu.SMEM),
            pl.BlockSpec((COMB_TM, D_MODEL), lambda i: (i, 0)),
            pl.BlockSpec(memory_space=pl.ANY),
            pl.BlockSpec((1, D_MODEL), lambda i: (0, 0)),
        ],
        out_specs=pl.BlockSpec((COMB_TM, D_MODEL), lambda i: (i, 0)),
        out_shape=jax.ShapeDtypeStruct((t, D_MODEL), F32),
        scratch_shapes=[pltpu.VMEM((2, 2 * COMB_TM, D_MODEL), F32), pltpu.SemaphoreType.DMA((2,))],
        compiler_params=_cparams(("arbitrary",)),
        name="combine",
    )(pos3, pos3, x1, y, g.reshape(1, D_MODEL))


def _route(logits, t):
    pg = jax.nn.softmax(logits[:, :N_GROUPS], axis=-1)
    grp = jnp.argmax(pg, axis=-1)
    p_grp = jnp.take_along_axis(pg, grp[:, None], axis=1)[:, 0]
    le = logits[:, N_GROUPS:N_GROUPS + N_EXPERTS].reshape(t, N_GROUPS, EXPERTS_PER_GROUP)
    le_g = jnp.take_along_axis(le, grp[:, None, None], axis=1)[:, 0]
    q_top, e_top = lax.top_k(jax.nn.softmax(le_g, axis=-1), TOPK_IN_GROUP)
    gates = p_grp[:, None] * q_top / jnp.sum(q_top, axis=-1, keepdims=True)
    expert = (grp[:, None] * EXPERTS_PER_GROUP + e_top).astype(jnp.int32)

    n_assign = t * TOPK_IN_GROUP
    e_flat = expert.reshape(-1)
    g_flat = gates.reshape(-1)
    tok_flat = jnp.repeat(jnp.arange(t, dtype=jnp.int32), TOPK_IN_GROUP)
    onehot = (e_flat[:, None] == jnp.arange(N_EXPERTS, dtype=jnp.int32)[None, :]).astype(jnp.int32)
    csum = jnp.cumsum(onehot, axis=0)
    rank = jnp.take_along_axis(csum, e_flat[:, None], axis=1)[:, 0] - 1
    counts = csum[-1]
    padded = (counts + FFN_TM - 1) // FFN_TM * FFN_TM
    end_p = jnp.cumsum(padded)
    start_p = end_p - padded
    dest = (start_p[e_flat] + rank).astype(jnp.int32)
    n_rows = n_assign + N_EXPERTS * FFN_TM
    n_blocks = n_rows // FFN_TM
    row_tok = jnp.zeros((n_rows,), jnp.int32).at[dest].set(tok_flat)
    row_w = jnp.zeros((n_rows,), F32).at[dest].set(g_flat)
    blk_start = jnp.arange(n_blocks, dtype=jnp.int32) * FFN_TM
    blk_expert = jnp.clip(jnp.searchsorted(end_p, blk_start, side='right'), 0, N_EXPERTS - 1).astype(jnp.int32)
    n_used = (end_p[-1] // FFN_TM).astype(jnp.int32).reshape(1)
    return blk_expert, n_used, row_tok, row_w, dest.reshape(t, TOPK_IN_GROUP)


def kernel(x, norm1_g, w_in, conv_a_w, conv_c_w, conv_c_b, ln_c_g, ln_c_b, w_out, norm2_g,
           router_g_w, router_g_b, router_e_w, router_e_b, w_gate, w_up, w_down, final_g):
    batch, seq, d = x.shape
    t = batch * seq
    depth = w_in.shape[0]
    xt = x.reshape(t, d)
    slopes = jnp.exp2(-8.0 * jnp.arange(1, N_HEADS + 1, dtype=F32) / N_HEADS)
    for l in range(depth):
        u = _in_proj(xt, norm1_g[l], w_in[l].astype(BF16))
        ya, yc = _mixers(u, conv_a_w[l], conv_c_w[l], conv_c_b[l], ln_c_g[l], ln_c_b[l], batch, seq)
        yb = _moba(u, slopes, batch, seq)
        wr = jnp.concatenate([router_g_w[l], router_e_w[l]], axis=1)
        wr = jnp.pad(wr, ((0, 0), (0, ROUTER_N - wr.shape[1]))).astype(BF16)
        br = jnp.concatenate([router_g_b[l], router_e_b[l]])
        br = jnp.pad(br, (0, ROUTER_N - br.shape[0])).reshape(1, ROUTER_N)
        x1, h2, logits = _out_proj(xt, ya, yb, yc, w_out[l].astype(BF16), norm2_g[l], wr, br)
        blk_expert, n_used, row_tok, row_w, pos = _route(logits, t)
        y = _ffn(h2, blk_expert, n_used, row_tok, row_w, w_gate[l], w_up[l], w_down[l])
        xt = _combine(x1, y, pos, final_g, final_norm=(l == depth - 1))
    return xt.reshape(batch, seq, d)
```

```python
import functools

import jax
import jax.numpy as jnp
from jax import lax
from jax.experimental import pallas as pl
from jax.experimental.pallas import tpu as pltpu

D_MODEL = 2048
D_CONV = 512
D_ATT = 1024
N_HEADS = 8
HEAD_DIM = 128
D_CFM = 512
D_MIX = D_CONV + D_ATT + D_CFM
D_IN = 3 * D_CONV + 3 * D_ATT + 2 * D_CFM
SHORT_CONV_W = 3
CFM_CONV_W = 31
MOBA_BLOCK = 256
MOBA_TOPK = 3
N_GROUPS = 4
EXPERTS_PER_GROUP = 8
N_EXPERTS = N_GROUPS * EXPERTS_PER_GROUP
TOPK_IN_GROUP = 2
D_EXPERT = 512
EPS = 1e-6
NEG = -1e30

LANES = 128
BF16_ROWS = 16
VMEM_LIMIT = 56 * 1024 * 1024

IN_TM = 1024
IN_TN = 512
MIX_TS = 256
HALO_A = 8
HALO_C = 32
MOBA_AUG = 2 * HEAD_DIM
OUT_TM = 256
ROUTER_N = LANES
FFN_TM = 256
COMB_TM = 256
RANK_TB = 256

BF16 = jnp.bfloat16
F32 = jnp.float32


def _cparams(sem):
    return pltpu.CompilerParams(dimension_semantics=sem, vmem_limit_bytes=VMEM_LIMIT)


def _in_proj_kernel(x_ref, g_ref, w_ref, o_ref, hn_ref):
    j = pl.program_id(1)

    @pl.when(j == 0)
    def _():
        x = x_ref[...]
        ms = jnp.mean(x * x, axis=-1, keepdims=True)
        hn_ref[...] = (x * lax.rsqrt(ms + EPS) * g_ref[...]).astype(BF16)

    q_lo = (3 * D_CONV) // IN_TN
    q_hi = (3 * D_CONV + D_ATT) // IN_TN
    col_scale = jnp.where((j >= q_lo) & (j < q_hi), HEAD_DIM ** -0.5, 1.0).astype(F32)
    acc = jnp.dot(hn_ref[...], w_ref[...], preferred_element_type=F32)
    o_ref[...] = (acc * col_scale).astype(o_ref.dtype)


def _in_proj(x, g, w_bf):
    t = x.shape[0]
    return pl.pallas_call(
        _in_proj_kernel,
        grid=(t // IN_TM, D_IN // IN_TN),
        in_specs=[
            pl.BlockSpec((IN_TM, D_MODEL), lambda i, j: (i, 0)),
            pl.BlockSpec((1, D_MODEL), lambda i, j: (0, 0)),
            pl.BlockSpec((D_MODEL, IN_TN), lambda i, j: (0, j)),
        ],
        out_specs=pl.BlockSpec((IN_TM, IN_TN), lambda i, j: (i, j)),
        out_shape=jax.ShapeDtypeStruct((t, D_IN), BF16),
        scratch_shapes=[pltpu.VMEM((IN_TM, D_MODEL), BF16)],
        compiler_params=_cparams(("arbitrary", "arbitrary")),
        name="in_proj",
    )(x, g.reshape(1, D_MODEL), w_bf)


def _mix_kernel(bg_ref, cg_ref, xv_ref, a_ref, gt_ref, wa_ref, wc_ref, bc_ref, lg_ref, lb_ref,
                ya_ref, yc_ref, bufa, bufc):
    s = pl.program_id(1)
    ts = MIX_TS

    @pl.when(s == 0)
    def _():
        bufa[0:HALO_A, :] = jnp.zeros((HALO_A, D_CONV), F32)
        bufc[0:HALO_C, :] = jnp.zeros((HALO_C, D_CFM), F32)

    @pl.when(s > 0)
    def _():
        bufa[0:HALO_A, :] = bufa[ts:ts + HALO_A, :]
        bufc[0:HALO_C, :] = bufc[ts:ts + HALO_C, :]

    bufa[HALO_A:HALO_A + ts, :] = cg_ref[...].astype(F32) * xv_ref[...].astype(F32)
    acc = jnp.zeros((ts, D_CONV), F32)
    for j in range(SHORT_CONV_W):
        off = HALO_A - (SHORT_CONV_W - 1) + j
        acc = acc + wa_ref[j:j + 1, :] * bufa[off:off + ts, :]
    ya_ref[...] = (bg_ref[...].astype(F32) * acc).astype(ya_ref.dtype)

    a = a_ref[...].astype(F32)
    gate = gt_ref[...].astype(F32)
    bufc[HALO_C:HALO_C + ts, :] = a * jax.nn.sigmoid(gate)
    acc = jnp.zeros((ts, D_CFM), F32) + bc_ref[...]
    for j in range(CFM_CONV_W):
        off = HALO_C - (CFM_CONV_W - 1) + j
        acc = acc + wc_ref[j:j + 1, :] * bufc[off:off + ts, :]
    mu = jnp.mean(acc, axis=-1, keepdims=True)
    cen = acc - mu
    var = jnp.mean(cen * cen, axis=-1, keepdims=True)
    hn = cen * lax.rsqrt(var + EPS) * lg_ref[...] + lb_ref[...]
    yc_ref[...] = (hn * jax.nn.sigmoid(hn)).astype(yc_ref.dtype)


def _mixers(u, wa, wc, bc, lg, lb, batch, seq):
    t = u.shape[0]
    nst = seq // MIX_TS
    row = lambda b, s: b * nst + s
    ublk = lambda c: pl.BlockSpec((MIX_TS, D_CONV), lambda b, s, c=c: (row(b, s), c))
    small = lambda r: pl.BlockSpec((r, D_CONV), lambda b, s: (0, 0))
    c0 = (3 * D_CONV + 3 * D_ATT) // D_CFM
    return pl.pallas_call(
        _mix_kernel,
        grid=(batch, nst),
        in_specs=[ublk(0), ublk(1), ublk(2), ublk(c0), ublk(c0 + 1),
                  small(SHORT_CONV_W), small(CFM_CONV_W), small(1), small(1), small(1)],
        out_specs=[pl.BlockSpec((MIX_TS, D_CONV), lambda b, s: (row(b, s), 0)),
                   pl.BlockSpec((MIX_TS, D_CFM), lambda b, s: (row(b, s), 0))],
        out_shape=[jax.ShapeDtypeStruct((t, D_CONV), BF16), jax.ShapeDtypeStruct((t, D_CFM), BF16)],
        scratch_shapes=[pltpu.VMEM((HALO_A + MIX_TS, D_CONV), F32),
                        pltpu.VMEM((HALO_C + MIX_TS, D_CFM), F32)],
        compiler_params=_cparams(("arbitrary", "arbitrary")),
        name="mixers",
    )(u, u, u, u, u, wa, wc, bc.reshape(1, D_CFM), lg.reshape(1, D_CFM), lb.reshape(1, D_CFM))


_NT = (((1,), (1,)), ((), ()))

F_MASK = HEAD_DIM
F_BIAS = HEAD_DIM + BF16_ROWS


def _moba_prepare(slope, q_ref, k_ref, v_ref, kaug, vt, qaug, ident):
    blk = MOBA_BLOCK
    seq = k_ref.shape[0]
    nb = seq // blk
    ri = lax.broadcasted_iota(jnp.int32, (blk, blk), 0)
    ci = lax.broadcasted_iota(jnp.int32, (blk, blk), 1)
    ident[...] = jnp.where(ri == ci, 1.0, 0.0).astype(BF16)
    eye_hd = ident[0:HEAD_DIM, 0:HEAD_DIM]

    kf = k_ref[...].astype(F32).reshape(nb, blk, HEAD_DIM)
    km = jnp.sum(kf, axis=1) * (1.0 / blk)
    km_hi = km.astype(BF16)
    km_lo = (km - km_hi.astype(F32)).astype(BF16)

    kaug[:, 0:HEAD_DIM] = k_ref[...]
    lane = lax.broadcasted_iota(jnp.int32, (blk, LANES), 1)
    cc = lax.broadcasted_iota(jnp.int32, (blk, LANES), 0).astype(F32)
    fb = F_BIAS - HEAD_DIM

    def fill(j, carry):
        start = pl.multiple_of(j * blk, blk)
        feat = jnp.where(lane == j, 1.0, 0.0)
        feat = jnp.where((lane == fb) | (lane == fb + 1), 1.0, feat)
        feat = jnp.where(lane == fb + 2, slope * cc, feat)
        feat = jnp.where(lane == fb + 3, slope * lax.convert_element_type(j * blk, F32), feat)
        kaug[pl.ds(start, blk), HEAD_DIM:MOBA_AUG] = feat.astype(BF16)
        vt[j] = lax.dot_general(eye_hd, v_ref[pl.ds(start, blk), :], _NT, preferred_element_type=F32).astype(BF16)
        return carry

    lax.fori_loop(0, nb, fill, 0)

    q = q_ref[...]
    qaug[0:HEAD_DIM, :] = lax.dot_general(eye_hd, q, _NT, preferred_element_type=F32).astype(BF16)

    gate = (lax.dot_general(km_hi, q, _NT, preferred_element_type=F32)
            + lax.dot_general(km_lo, q, _NT, preferred_element_type=F32))
    sub = lax.broadcasted_iota(jnp.int32, (nb, seq), 0)
    tpos = lax.broadcasted_iota(jnp.int32, (nb, seq), 1)
    qblk = lax.shift_right_logical(tpos, blk.bit_length() - 1)
    rank = jnp.zeros((nb, seq), F32)
    for jp in range(nb):
        row = gate[jp:jp + 1, :]
        beats = (row > gate) | ((row == gate) & (sub > jp))
        rank = rank + jnp.where(beats & (qblk > jp), 1.0, 0.0)
    allowed = ((sub < qblk) & (rank < MOBA_TOPK)) | (sub == qblk)
    qaug[F_MASK:F_MASK + BF16_ROWS, :] = jnp.where(allowed, 0.0, NEG).astype(BF16)

    rq = (tpos & (blk - 1)).astype(F32)
    bias = jnp.where(sub == 0, -slope * rq, 0.0)
    bias = jnp.where(sub == 1, -slope * (qblk * blk).astype(F32), bias)
    bias = jnp.where((sub == 2) | (sub == 3), 1.0, bias)
    qaug[F_BIAS:F_BIAS + BF16_ROWS, :] = bias.astype(BF16)
    qaug[F_BIAS + BF16_ROWS:MOBA_AUG, :] = jnp.zeros((MOBA_AUG - F_BIAS - BF16_ROWS, seq), BF16)


def _moba_tile(n, kaug, vt, qaug, ident, o_ref):
    blk = MOBA_BLOCK
    qa = qaug[:, n * blk:(n + 1) * blk]
    kc = lax.broadcasted_iota(jnp.int32, (blk, blk), 0)
    qr = lax.broadcasted_iota(jnp.int32, (blk, blk), 1)
    ss = [jnp.dot(kaug[j * blk:(j + 1) * blk, :], qa, preferred_element_type=F32) for j in range(n + 1)]
    ss[n] = jnp.where(kc <= qr, ss[n], NEG)
    m = jnp.max(ss[0], axis=0, keepdims=True)
    for j in range(1, n + 1):
        m = jnp.maximum(m, jnp.max(ss[j], axis=0, keepdims=True))
    l = jnp.zeros((1, blk), F32)
    acc = jnp.zeros((HEAD_DIM, blk), F32)
    for j in range(n + 1):
        p = jnp.exp(ss[j] - m)
        l = l + jnp.sum(p, axis=0, keepdims=True)
        acc = acc + jnp.dot(vt[j], p.astype(BF16), preferred_element_type=F32)
    out_t = (acc / l).astype(BF16)
    o_ref[...] = lax.dot_general(ident[...], out_t, _NT, preferred_element_type=F32).astype(o_ref.dtype)


def _moba_kernel(slopes_ref, q_ref, k_ref, v_ref, o_ref, kaug, vt, qaug, ident):
    qi = pl.program_id(2)

    @pl.when(qi == 0)
    def _():
        _moba_prepare(slopes_ref[pl.program_id(1)], q_ref, k_ref, v_ref, kaug, vt, qaug, ident)

    for n in range(k_ref.shape[0] // MOBA_BLOCK):
        @pl.when(qi == n)
        def _(n=n):
            _moba_tile(n, kaug, vt, qaug, ident, o_ref)


def _moba(u, slopes, batch, seq):
    t = u.shape[0]
    nq = seq // MOBA_BLOCK
    assert nq == BF16_ROWS
    qc = (3 * D_CONV) // HEAD_DIM
    kc = qc + N_HEADS
    vc = kc + N_HEADS
    whole = lambda c0: pl.BlockSpec((seq, HEAD_DIM), lambda b, h, i, s: (b, c0 + h))
    return pl.pallas_call(
        _moba_kernel,
        grid_spec=pltpu.PrefetchScalarGridSpec(
            num_scalar_prefetch=1,
            grid=(batch, N_HEADS, nq),
            in_specs=[whole(qc), whole(kc), whole(vc)],
            out_specs=pl.BlockSpec((MOBA_BLOCK, HEAD_DIM), lambda b, h, i, s: (b * nq + i, h)),
            scratch_shapes=[
                pltpu.VMEM((seq, MOBA_AUG), BF16),
                pltpu.VMEM((nq, HEAD_DIM, MOBA_BLOCK), BF16),
                pltpu.VMEM((MOBA_AUG, seq), BF16),
                pltpu.VMEM((MOBA_BLOCK, MOBA_BLOCK), BF16),
            ],
        ),
        out_shape=jax.ShapeDtypeStruct((t, D_ATT), BF16),
        compiler_params=_cparams(("arbitrary", "arbitrary", "arbitrary")),
        name="moba",
    )(slopes, u, u, u)


def _out_proj_kernel(x_ref, ya_ref, yb_ref, yc_ref, w_ref, g_ref, wr_ref, br_ref, x1_ref, h2_ref, lg_ref):
    acc = x_ref[...]
    acc = acc + jnp.dot(ya_ref[...], w_ref[0:D_CONV, :], preferred_element_type=F32)
    acc = acc + jnp.dot(yb_ref[...], w_ref[D_CONV:D_CONV + D_ATT, :], preferred_element_type=F32)
    acc = acc + jnp.dot(yc_ref[...], w_ref[D_CONV + D_ATT:D_MIX, :], preferred_element_type=F32)
    x1_ref[...] = acc
    ms = jnp.mean(acc * acc, axis=-1, keepdims=True)
    h2 = acc * lax.rsqrt(ms + EPS) * g_ref[...]
    h2_ref[...] = h2
    lg_ref[...] = jnp.dot(h2.astype(BF16), wr_ref[...], preferred_element_type=F32) + br_ref[...]


def _out_proj(x, ya, yb, yc, w_bf, g, wr_bf, br):
    t = x.shape[0]
    rows = lambda n: pl.BlockSpec((OUT_TM, n), lambda i: (i, 0))
    whole = lambda a, b: pl.BlockSpec((a, b), lambda i: (0, 0))
    return pl.pallas_call(
        _out_proj_kernel,
        grid=(t // OUT_TM,),
        in_specs=[rows(D_MODEL), rows(D_CONV), rows(D_ATT), rows(D_CFM),
                  whole(D_MIX, D_MODEL), whole(1, D_MODEL), whole(D_MODEL, ROUTER_N), whole(1, ROUTER_N)],
        out_specs=[rows(D_MODEL), rows(D_MODEL), rows(ROUTER_N)],
        out_shape=[jax.ShapeDtypeStruct((t, D_MODEL), F32), jax.ShapeDtypeStruct((t, D_MODEL), F32),
                   jax.ShapeDtypeStruct((t, ROUTER_N), F32)],
        compiler_params=_cparams(("arbitrary",)),
        name="out_proj",
    )(x, ya, yb, yc, w_bf, g.reshape(1, D_MODEL), wr_bf, br)


def _gather_rows(idx_ref, n, src_hbm, dst, sem):
    def body(r, carry):
        pltpu.make_async_copy(src_hbm.at[pl.ds(idx_ref[0, 0, r], 1)], dst.at[pl.ds(r, 1)], sem).start()
        return carry
    lax.fori_loop(0, n, body, 0, unroll=8)


def _wait_rows(n, src_hbm, dst, sem):
    pltpu.make_async_copy(src_hbm.at[pl.ds(0, n)], dst, sem).wait()


def _ffn_kernel(be_ref, nb_ref, tok_ref, tokn_ref, h2_hbm, wg_ref, wu_ref, wd_ref, rw_ref, y_ref,
                xbuf, sem, wgb, wub, wdb):
    i = pl.program_id(0)
    nb = nb_ref[0]
    slot = lax.rem(i, 2)

    @pl.when(i == 0)
    def _():
        _gather_rows(tok_ref, FFN_TM, h2_hbm, xbuf.at[0], sem.at[0])

    @pl.when(i + 1 < nb)
    def _():
        _gather_rows(tokn_ref, FFN_TM, h2_hbm, xbuf.at[1 - slot], sem.at[1 - slot])

    @pl.when(i < nb)
    def _():
        prev = be_ref[jnp.maximum(i - 1, 0)]

        @pl.when((i == 0) | (be_ref[i] != prev))
        def _():
            wgb[...] = wg_ref[...].astype(BF16)
            wub[...] = wu_ref[...].astype(BF16)
            wdb[...] = wd_ref[...].astype(BF16)

        _wait_rows(FFN_TM, h2_hbm, xbuf.at[slot], sem.at[slot])
        xb = xbuf[slot].astype(BF16)
        g = jnp.dot(xb, wgb[...], preferred_element_type=F32)
        up = jnp.dot(xb, wub[...], preferred_element_type=F32)
        hid = (g * jax.nn.sigmoid(g) * up).astype(BF16)
        y = jnp.dot(hid, wdb[...], preferred_element_type=F32)
        y_ref[...] = y * rw_ref[...]

    @pl.when(i >= nb)
    def _():
        y_ref[...] = jnp.zeros(y_ref.shape, y_ref.dtype)


def _ffn(h2, blk_expert, n_used, row_tok, row_w, w_gate, w_up, w_down, layer):
    n_rows = row_tok.shape[0]
    n_blocks = n_rows // FFN_TM
    tok3 = row_tok.reshape(n_blocks, 1, FFN_TM)
    wspec = lambda a, b: pl.BlockSpec((None, None, a, b), lambda i, be, nb: (layer, be[i], 0, 0))
    return pl.pallas_call(
        _ffn_kernel,
        grid_spec=pltpu.PrefetchScalarGridSpec(
            num_scalar_prefetch=2,
            grid=(n_blocks,),
            in_specs=[
                pl.BlockSpec((1, 1, FFN_TM), lambda i, be, nb: (i, 0, 0), memory_space=pltpu.SMEM),
                pl.BlockSpec((1, 1, FFN_TM), lambda i, be, nb: (jnp.minimum(i + 1, n_blocks - 1), 0, 0),
                             memory_space=pltpu.SMEM),
                pl.BlockSpec(memory_space=pl.ANY),
                wspec(D_MODEL, D_EXPERT), wspec(D_MODEL, D_EXPERT), wspec(D_EXPERT, D_MODEL),
                pl.BlockSpec((FFN_TM, 1), lambda i, be, nb: (i, 0)),
            ],
            out_specs=pl.BlockSpec((FFN_TM, D_MODEL), lambda i, be, nb: (i, 0)),
            scratch_shapes=[
                pltpu.VMEM((2, FFN_TM, D_MODEL), F32),
                pltpu.SemaphoreType.DMA((2,)),
                pltpu.VMEM((D_MODEL, D_EXPERT), BF16),
                pltpu.VMEM((D_MODEL, D_EXPERT), BF16),
                pltpu.VMEM((D_EXPERT, D_MODEL), BF16),
            ],
        ),
        out_shape=jax.ShapeDtypeStruct((n_rows, D_MODEL), F32),
        compiler_params=_cparams(("arbitrary",)),
        name="ffn",
    )(blk_expert, n_used, tok3, tok3, h2, w_gate, w_up, w_down, row_w.reshape(n_rows, 1))


def _combine_kernel(pos_ref, posn_ref, x1_ref, y_hbm, g_ref, o_ref, ybuf, sem, *, final_norm):
    i = pl.program_id(0)
    n = pl.num_programs(0)
    slot = lax.rem(i, 2)

    @pl.when(i == 0)
    def _():
        _gather_rows(pos_ref, 2 * COMB_TM, y_hbm, ybuf.at[0], sem.at[0])

    @pl.when(i + 1 < n)
    def _():
        _gather_rows(posn_ref, 2 * COMB_TM, y_hbm, ybuf.at[1 - slot], sem.at[1 - slot])

    _wait_rows(2 * COMB_TM, y_hbm, ybuf.at[slot], sem.at[slot])
    out = x1_ref[...] + (ybuf[slot, 0:COMB_TM, :] + ybuf[slot, COMB_TM:2 * COMB_TM, :])
    if final_norm:
        ms = jnp.mean(out * out, axis=-1, keepdims=True)
        out = out * lax.rsqrt(ms + EPS) * g_ref[...]
    o_ref[...] = out


def _combine(x1, y, pos, g, final_norm):
    t = x1.shape[0]
    nt = t // COMB_TM
    pos3 = pos.reshape(nt, COMB_TM, TOPK_IN_GROUP).transpose(0, 2, 1).reshape(nt, 1, 2 * COMB_TM)
    return pl.pallas_call(
        functools.partial(_combine_kernel, final_norm=final_norm),
        grid=(nt,),
        in_specs=[
            pl.BlockSpec((1, 1, 2 * COMB_TM), lambda i: (i, 0, 0), memory_space=pltpu.SMEM),
            pl.BlockSpec((1, 1, 2 * COMB_TM), lambda i: (jnp.minimum(i + 1, nt - 1), 0, 0),
                         memory_space=pltpu.SMEM),
            pl.BlockSpec((COMB_TM, D_MODEL), lambda i: (i, 0)),
            pl.BlockSpec(memory_space=pl.ANY),
            pl.BlockSpec((1, D_MODEL), lambda i: (0, 0)),
        ],
        out_specs=pl.BlockSpec((COMB_TM, D_MODEL), lambda i: (i, 0)),
        out_shape=jax.ShapeDtypeStruct((t, D_MODEL), F32),
        scratch_shapes=[pltpu.VMEM((2, 2 * COMB_TM, D_MODEL), F32), pltpu.SemaphoreType.DMA((2,))],
        compiler_params=_cparams(("arbitrary",)),
        name="combine",
    )(pos3, pos3, x1, y, g.reshape(1, D_MODEL))


def _route(logits, t):
    pg = jax.nn.softmax(logits[:, :N_GROUPS], axis=-1)
    grp = jnp.argmax(pg, axis=-1)
    p_grp = jnp.take_along_axis(pg, grp[:, None], axis=1)[:, 0]
    le = logits[:, N_GROUPS:N_GROUPS + N_EXPERTS].reshape(t, N_GROUPS, EXPERTS_PER_GROUP)
    le_g = jnp.take_along_axis(le, grp[:, None, None], axis=1)[:, 0]
    q_top, e_top = lax.top_k(jax.nn.softmax(le_g, axis=-1), TOPK_IN_GROUP)
    gates = p_grp[:, None] * q_top / jnp.sum(q_top, axis=-1, keepdims=True)
    expert = (grp[:, None] * EXPERTS_PER_GROUP + e_top).astype(jnp.int32)

    n_assign = t * TOPK_IN_GROUP
    e_flat = expert.reshape(-1)
    g_flat = gates.reshape(-1)
    tok_flat = jnp.repeat(jnp.arange(t, dtype=jnp.int32), TOPK_IN_GROUP)
    onehot = (e_flat[:, None] == jnp.arange(N_EXPERTS, dtype=jnp.int32)[None, :]).astype(F32)
    oh3 = onehot.reshape(n_assign // RANK_TB, RANK_TB, N_EXPERTS)
    tri = (jnp.arange(RANK_TB)[:, None] > jnp.arange(RANK_TB)[None, :]).astype(F32)
    within = jnp.einsum('ij,bjk->bik', tri, oh3, precision=lax.Precision.HIGHEST)
    blk_tot = jnp.sum(oh3, axis=1)
    blk_off = jnp.cumsum(blk_tot, axis=0) - blk_tot
    before = (within + blk_off[:, None, :]).reshape(n_assign, N_EXPERTS)
    rank = jnp.sum(before * onehot, axis=1).astype(jnp.int32)
    counts = jnp.sum(blk_tot, axis=0).astype(jnp.int32)
    padded = (counts + FFN_TM - 1) // FFN_TM * FFN_TM
    end_p = jnp.cumsum(padded)
    start_p = end_p - padded
    dest = (start_p[e_flat] + rank).astype(jnp.int32)
    n_rows = n_assign + N_EXPERTS * FFN_TM
    n_blocks = n_rows // FFN_TM
    row_tok = jnp.zeros((n_rows,), jnp.int32).at[dest].set(tok_flat)
    row_w = jnp.zeros((n_rows,), F32).at[dest].set(g_flat)
    blk_start = jnp.arange(n_blocks, dtype=jnp.int32) * FFN_TM
    blk_expert = jnp.clip(jnp.searchsorted(end_p, blk_start, side='right'), 0, N_EXPERTS - 1).astype(jnp.int32)
    n_used = (end_p[-1] // FFN_TM).astype(jnp.int32).reshape(1)
    return blk_expert, n_used, row_tok, row_w, dest.reshape(t, TOPK_IN_GROUP)


def kernel(x, norm1_g, w_in, conv_a_w, conv_c_w, conv_c_b, ln_c_g, ln_c_b, w_out, norm2_g,
           router_g_w, router_g_b, router_e_w, router_e_b, w_gate, w_up, w_down, final_g):
    batch, seq, d = x.shape
    t = batch * seq
    depth = w_in.shape[0]
    xt = x.reshape(t, d)
    slopes = jnp.exp2(-8.0 * jnp.arange(1, N_HEADS + 1, dtype=F32) / N_HEADS)
    for l in range(depth):
        u = _in_proj(xt, norm1_g[l], w_in[l].astype(BF16))
        ya, yc = _mixers(u, conv_a_w[l], conv_c_w[l], conv_c_b[l], ln_c_g[l], ln_c_b[l], batch, seq)
        yb = _moba(u, slopes, batch, seq)
        wr = jnp.concatenate([router_g_w[l], router_e_w[l]], axis=1)
        wr = jnp.pad(wr, ((0, 0), (0, ROUTER_N - wr.shape[1]))).astype(BF16)
        br = jnp.concatenate([router_g_b[l], router_e_b[l]])
        br = jnp.pad(br, (0, ROUTER_N - br.shape[0])).reshape(1, ROUTER_N)
        x1, h2, logits = _out_proj(xt, ya, yb, yc, w_out[l].astype(BF16), norm2_g[l], wr, br)
        blk_expert, n_used, row_tok, row_w, pos = _route(logits, t)
        y = _ffn(h2, blk_expert, n_used, row_tok, row_w, w_gate, w_up, w_down, l)
        xt = _combine(x1, y, pos, final_g, final_norm=(l == depth - 1))
    return xt.reshape(batch, seq, d)
```

```python
import functools

import jax
import jax.numpy as jnp
from jax import lax
from jax.experimental import pallas as pl
from jax.experimental.pallas import tpu as pltpu

D_MODEL = 2048
D_CONV = 512
D_ATT = 1024
N_HEADS = 8
HEAD_DIM = 128
D_CFM = 512
D_MIX = D_CONV + D_ATT + D_CFM
D_IN = 3 * D_CONV + 3 * D_ATT + 2 * D_CFM
SHORT_CONV_W = 3
CFM_CONV_W = 31
MOBA_BLOCK = 256
MOBA_TOPK = 3
N_GROUPS = 4
EXPERTS_PER_GROUP = 8
N_EXPERTS = N_GROUPS * EXPERTS_PER_GROUP
TOPK_IN_GROUP = 2
D_EXPERT = 512
EPS = 1e-6
NEG = -1e30

LANES = 128
SUBLANES = 8
BF16_ROWS = 16
VMEM_LIMIT = 56 * 1024 * 1024

IN_TM = 1024
IN_TN = 512
MIX_TS = 256
HALO_A = 8
HALO_C = 32
MOBA_AUG = 2 * HEAD_DIM
OUT_TM = 256
ROUTER_ROWS = LANES
ROUTER_E0 = SUBLANES
FFN_TM = 256
TOK_TM = 256

BF16 = jnp.bfloat16
F32 = jnp.float32
I32 = jnp.int32


def _cparams(sem):
    return pltpu.CompilerParams(dimension_semantics=sem, vmem_limit_bytes=VMEM_LIMIT)


def _in_proj_kernel(x_ref, g_ref, w_ref, o_ref, hn_ref):
    j = pl.program_id(1)

    @pl.when(j == 0)
    def _():
        x = x_ref[...]
        ms = jnp.mean(x * x, axis=-1, keepdims=True)
        hn_ref[...] = (x * lax.rsqrt(ms + EPS) * g_ref[...]).astype(BF16)

    q_lo = (3 * D_CONV) // IN_TN
    q_hi = (3 * D_CONV + D_ATT) // IN_TN
    col_scale = jnp.where((j >= q_lo) & (j < q_hi), HEAD_DIM ** -0.5, 1.0).astype(F32)
    acc = jnp.dot(hn_ref[...], w_ref[...], preferred_element_type=F32)
    o_ref[...] = (acc * col_scale).astype(o_ref.dtype)


def _in_proj(x, g, w_bf):
    t = x.shape[0]
    return pl.pallas_call(
        _in_proj_kernel,
        grid=(t // IN_TM, D_IN // IN_TN),
        in_specs=[
            pl.BlockSpec((IN_TM, D_MODEL), lambda i, j: (i, 0)),
            pl.BlockSpec((1, D_MODEL), lambda i, j: (0, 0)),
            pl.BlockSpec((D_MODEL, IN_TN), lambda i, j: (0, j)),
        ],
        out_specs=pl.BlockSpec((IN_TM, IN_TN), lambda i, j: (i, j)),
        out_shape=jax.ShapeDtypeStruct((t, D_IN), BF16),
        scratch_shapes=[pltpu.VMEM((IN_TM, D_MODEL), BF16)],
        compiler_params=_cparams(("arbitrary", "arbitrary")),
        name="in_proj",
    )(x, g.reshape(1, D_MODEL), w_bf)


def _mix_kernel(bg_ref, cg_ref, xv_ref, a_ref, gt_ref, wa_ref, wc_ref, bc_ref, lg_ref, lb_ref,
                ya_ref, yc_ref, bufa, bufc):
    s = pl.program_id(1)
    ts = MIX_TS

    @pl.when(s == 0)
    def _():
        bufa[0:HALO_A, :] = jnp.zeros((HALO_A, D_CONV), F32)
        bufc[0:HALO_C, :] = jnp.zeros((HALO_C, D_CFM), F32)

    @pl.when(s > 0)
    def _():
        bufa[0:HALO_A, :] = bufa[ts:ts + HALO_A, :]
        bufc[0:HALO_C, :] = bufc[ts:ts + HALO_C, :]

    bufa[HALO_A:HALO_A + ts, :] = cg_ref[...].astype(F32) * xv_ref[...].astype(F32)
    acc = jnp.zeros((ts, D_CONV), F32)
    for j in range(SHORT_CONV_W):
        off = HALO_A - (SHORT_CONV_W - 1) + j
        acc = acc + wa_ref[j:j + 1, :] * bufa[off:off + ts, :]
    ya_ref[...] = (bg_ref[...].astype(F32) * acc).astype(ya_ref.dtype)

    a = a_ref[...].astype(F32)
    gate = gt_ref[...].astype(F32)
    bufc[HALO_C:HALO_C + ts, :] = a * jax.nn.sigmoid(gate)
    acc = jnp.zeros((ts, D_CFM), F32) + bc_ref[...]
    for j in range(CFM_CONV_W):
        off = HALO_C - (CFM_CONV_W - 1) + j
        acc = acc + wc_ref[j:j + 1, :] * bufc[off:off + ts, :]
    mu = jnp.mean(acc, axis=-1, keepdims=True)
    cen = acc - mu
    var = jnp.mean(cen * cen, axis=-1, keepdims=True)
    hn = cen * lax.rsqrt(var + EPS) * lg_ref[...] + lb_ref[...]
    yc_ref[...] = (hn * jax.nn.sigmoid(hn)).astype(yc_ref.dtype)


def _mixers(u, wa, wc, bc, lg, lb, batch, seq):
    t = u.shape[0]
    nst = seq // MIX_TS
    row = lambda b, s: b * nst + s
    ublk = lambda c: pl.BlockSpec((MIX_TS, D_CONV), lambda b, s, c=c: (row(b, s), c))
    small = lambda r: pl.BlockSpec((r, D_CONV), lambda b, s: (0, 0))
    c0 = (3 * D_CONV + 3 * D_ATT) // D_CFM
    return pl.pallas_call(
        _mix_kernel,
        grid=(batch, nst),
        in_specs=[ublk(0), ublk(1), ublk(2), ublk(c0), ublk(c0 + 1),
                  small(SHORT_CONV_W), small(CFM_CONV_W), small(1), small(1), small(1)],
        out_specs=[pl.BlockSpec((MIX_TS, D_CONV), lambda b, s: (row(b, s), 0)),
                   pl.BlockSpec((MIX_TS, D_CFM), lambda b, s: (row(b, s), 0))],
        out_shape=[jax.ShapeDtypeStruct((t, D_CONV), BF16), jax.ShapeDtypeStruct((t, D_CFM), BF16)],
        scratch_shapes=[pltpu.VMEM((HALO_A + MIX_TS, D_CONV), F32),
                        pltpu.VMEM((HALO_C + MIX_TS, D_CFM), F32)],
        compiler_params=_cparams(("arbitrary", "arbitrary")),
        name="mixers",
    )(u, u, u, u, u, wa, wc, bc.reshape(1, D_CFM), lg.reshape(1, D_CFM), lb.reshape(1, D_CFM))


_NT = (((1,), (1,)), ((), ()))

F_MASK = HEAD_DIM
F_BIAS = HEAD_DIM + BF16_ROWS


def _moba_prepare(slope, q_ref, k_ref, v_ref, kaug, vt, qaug, ident):
    blk = MOBA_BLOCK
    seq = k_ref.shape[0]
    nb = seq // blk
    ri = lax.broadcasted_iota(I32, (blk, blk), 0)
    ci = lax.broadcasted_iota(I32, (blk, blk), 1)
    ident[...] = jnp.where(ri == ci, 1.0, 0.0).astype(BF16)
    eye_hd = ident[0:HEAD_DIM, 0:HEAD_DIM]

    kf = k_ref[...].astype(F32).reshape(nb, blk, HEAD_DIM)
    km = jnp.sum(kf, axis=1) * (1.0 / blk)
    km_hi = km.astype(BF16)
    km_lo = (km - km_hi.astype(F32)).astype(BF16)

    kaug[:, 0:HEAD_DIM] = k_ref[...]
    lane = lax.broadcasted_iota(I32, (blk, LANES), 1)
    cc = lax.broadcasted_iota(I32, (blk, LANES), 0).astype(F32)
    fb = F_BIAS - HEAD_DIM

    def fill(j, carry):
        start = pl.multiple_of(j * blk, blk)
        feat = jnp.where(lane == j, 1.0, 0.0)
        feat = jnp.where((lane == fb) | (lane == fb + 1), 1.0, feat)
        feat = jnp.where(lane == fb + 2, slope * cc, feat)
        feat = jnp.where(lane == fb + 3, slope * lax.convert_element_type(j * blk, F32), feat)
        kaug[pl.ds(start, blk), HEAD_DIM:MOBA_AUG] = feat.astype(BF16)
        vt[j] = lax.dot_general(eye_hd, v_ref[pl.ds(start, blk), :], _NT, preferred_element_type=F32).astype(BF16)
        return carry

    lax.fori_loop(0, nb, fill, 0)

    q = q_ref[...]
    qaug[0:HEAD_DIM, :] = lax.dot_general(eye_hd, q, _NT, preferred_element_type=F32).astype(BF16)

    gate = (lax.dot_general(km_hi, q, _NT, preferred_element_type=F32)
            + lax.dot_general(km_lo, q, _NT, preferred_element_type=F32))
    sub = lax.broadcasted_iota(I32, (nb, seq), 0)
    tpos = lax.broadcasted_iota(I32, (nb, seq), 1)
    qblk = lax.shift_right_logical(tpos, blk.bit_length() - 1)
    rank = jnp.zeros((nb, seq), F32)
    for jp in range(nb):
        row = gate[jp:jp + 1, :]
        beats = (row > gate) | ((row == gate) & (sub > jp))
        rank = rank + jnp.where(beats & (qblk > jp), 1.0, 0.0)
    allowed = ((sub < qblk) & (rank < MOBA_TOPK)) | (sub == qblk)
    qaug[F_MASK:F_MASK + BF16_ROWS, :] = jnp.where(allowed, 0.0, NEG).astype(BF16)

    rq = (tpos & (blk - 1)).astype(F32)
    bias = jnp.where(sub == 0, -slope * rq, 0.0)
    bias = jnp.where(sub == 1, -slope * (qblk * blk).astype(F32), bias)
    bias = jnp.where((sub == 2) | (sub == 3), 1.0, bias)
    qaug[F_BIAS:F_BIAS + BF16_ROWS, :] = bias.astype(BF16)
    qaug[F_BIAS + BF16_ROWS:MOBA_AUG, :] = jnp.zeros((MOBA_AUG - F_BIAS - BF16_ROWS, seq), BF16)


def _moba_tile(n, kaug, vt, qaug, ident, o_ref):
    blk = MOBA_BLOCK
    qa = qaug[:, n * blk:(n + 1) * blk]
    kc = lax.broadcasted_iota(I32, (blk, blk), 0)
    qr = lax.broadcasted_iota(I32, (blk, blk), 1)
    ss = [jnp.dot(kaug[j * blk:(j + 1) * blk, :], qa, preferred_element_type=F32) for j in range(n + 1)]
    ss[n] = jnp.where(kc <= qr, ss[n], NEG)
    m = jnp.max(ss[0], axis=0, keepdims=True)
    for j in range(1, n + 1):
        m = jnp.maximum(m, jnp.max(ss[j], axis=0, keepdims=True))
    l = jnp.zeros((1, blk), F32)
    acc = jnp.zeros((HEAD_DIM, blk), F32)
    for j in range(n + 1):
        p = jnp.exp(ss[j] - m)
        l = l + jnp.sum(p, axis=0, keepdims=True)
        acc = acc + jnp.dot(vt[j], p.astype(BF16), preferred_element_type=F32)
    out_t = (acc / l).astype(BF16)
    o_ref[...] = lax.dot_general(ident[...], out_t, _NT, preferred_element_type=F32).astype(o_ref.dtype)


def _moba_kernel(slopes_ref, q_ref, k_ref, v_ref, o_ref, kaug, vt, qaug, ident):
    qi = pl.program_id(2)

    @pl.when(qi == 0)
    def _():
        _moba_prepare(slopes_ref[pl.program_id(1)], q_ref, k_ref, v_ref, kaug, vt, qaug, ident)

    for n in range(k_ref.shape[0] // MOBA_BLOCK):
        @pl.when(qi == n)
        def _(n=n):
            _moba_tile(n, kaug, vt, qaug, ident, o_ref)


def _moba(u, slopes, batch, seq):
    t = u.shape[0]
    nq = seq // MOBA_BLOCK
    assert nq == BF16_ROWS
    qc = (3 * D_CONV) // HEAD_DIM
    kc = qc + N_HEADS
    vc = kc + N_HEADS
    whole = lambda c0: pl.BlockSpec((seq, HEAD_DIM), lambda b, h, i, s: (b, c0 + h))
    return pl.pallas_call(
        _moba_kernel,
        grid_spec=pltpu.PrefetchScalarGridSpec(
            num_scalar_prefetch=1,
            grid=(batch, N_HEADS, nq),
            in_specs=[whole(qc), whole(kc), whole(vc)],
            out_specs=pl.BlockSpec((MOBA_BLOCK, HEAD_DIM), lambda b, h, i, s: (b * nq + i, h)),
            scratch_shapes=[
                pltpu.VMEM((seq, MOBA_AUG), BF16),
                pltpu.VMEM((nq, HEAD_DIM, MOBA_BLOCK), BF16),
                pltpu.VMEM((MOBA_AUG, seq), BF16),
                pltpu.VMEM((MOBA_BLOCK, MOBA_BLOCK), BF16),
            ],
        ),
        out_shape=jax.ShapeDtypeStruct((t, D_ATT), BF16),
        compiler_params=_cparams(("arbitrary", "arbitrary", "arbitrary")),
        name="moba",
    )(slopes, u, u, u)


def _first_max(v, rows):
    top = jnp.max(v, axis=0, keepdims=True)
    idx = jnp.min(jnp.where(v == top, rows, float(v.shape[0])), axis=0, keepdims=True)
    return top, idx


def _router(lg, tri_ref, carry_ref):
    tm = lg.shape[1]
    rows = lax.broadcasted_iota(I32, (SUBLANES, tm), 0).astype(F32)
    grp_lg = jnp.where(rows < N_GROUPS, lg[0:SUBLANES, :], NEG)
    ex = jnp.exp(grp_lg - jnp.max(grp_lg, axis=0, keepdims=True))
    pg = ex / jnp.sum(ex, axis=0, keepdims=True)
    p_grp, grp = _first_max(pg, rows)
    le = jnp.zeros((EXPERTS_PER_GROUP, tm), F32)
    for g in range(N_GROUPS):
        r0 = ROUTER_E0 + g * EXPERTS_PER_GROUP
        le = jnp.where(grp == g, lg[r0:r0 + EXPERTS_PER_GROUP, :], le)
    ex = jnp.exp(le - jnp.max(le, axis=0, keepdims=True))
    q = ex / jnp.sum(ex, axis=0, keepdims=True)
    q1, i1 = _first_max(q, rows)
    q2, i2 = _first_max(jnp.where(rows == i1, -1.0, q), rows)
    den = q1 + q2
    g0 = p_grp * q1 / den
    g1 = p_grp * q2 / den
    e0 = grp * EXPERTS_PER_GROUP + i1
    e1 = grp * EXPERTS_PER_GROUP + i2

    erow = lax.broadcasted_iota(I32, (N_EXPERTS, tm), 0).astype(F32)
    is0 = erow == e0
    is1 = erow == e1
    onehot = jnp.where(is0 | is1, 1.0, 0.0)
    before = jnp.dot(onehot.astype(BF16), tri_ref[...], preferred_element_type=F32) + carry_ref[:, 0:1]
    r0 = jnp.sum(jnp.where(is0, before, 0.0), axis=0, keepdims=True)
    r1 = jnp.sum(jnp.where(is1, before, 0.0), axis=0, keepdims=True)
    carry_ref[...] = carry_ref[...] + jnp.sum(onehot, axis=1, keepdims=True)
    return e0, e1, g0, g1, r0, r1


def _out_proj_kernel(x_ref, ya_ref, yb_ref, yc_ref, w_ref, g_ref, wrt_ref, brt_ref,
                     x1_ref, h2_ref, ri_ref, rf_ref, cnt_ref, tri_ref, carry_ref):
    tm = OUT_TM

    @pl.when(pl.program_id(0) == 0)
    def _():
        a = lax.broadcasted_iota(I32, (tm, tm), 0)
        b = lax.broadcasted_iota(I32, (tm, tm), 1)
        tri_ref[...] = jnp.where(a < b, 1.0, 0.0).astype(BF16)
        carry_ref[...] = jnp.zeros(carry_ref.shape, F32)

    acc = x_ref[...]
    acc = acc + jnp.dot(ya_ref[...], w_ref[0:D_CONV, :], preferred_element_type=F32)
    acc = acc + jnp.dot(yb_ref[...], w_ref[D_CONV:D_CONV + D_ATT, :], preferred_element_type=F32)
    acc = acc + jnp.dot(yc_ref[...], w_ref[D_CONV + D_ATT:D_MIX, :], preferred_element_type=F32)
    x1_ref[...] = acc
    ms = jnp.mean(acc * acc, axis=-1, keepdims=True)
    h2 = acc * lax.rsqrt(ms + EPS) * g_ref[...]
    h2_ref[...] = h2

    lg = lax.dot_general(wrt_ref[...], h2.astype(BF16), _NT, preferred_element_type=F32) + brt_ref[...]
    e0, e1, g0, g1, r0, r1 = _router(lg, tri_ref, carry_ref)
    row = lax.broadcasted_iota(I32, (SUBLANES, tm), 0)
    ints = jnp.where(row == 0, e0, jnp.where(row == 1, e1, jnp.where(row == 2, r0, jnp.where(row == 3, r1, 0.0))))
    ri_ref[...] = ints.astype(I32)
    rf_ref[...] = jnp.where(row == 0, g0, jnp.where(row == 1, g1, 0.0))
    cnt_ref[...] = carry_ref[...]


def _out_proj(x, ya, yb, yc, w_bf, g, wrt_bf, brt):
    t = x.shape[0]
    rows = lambda n: pl.BlockSpec((OUT_TM, n), lambda i: (i, 0))
    whole = lambda a, b: pl.BlockSpec((a, b), lambda i: (0, 0))
    cols = pl.BlockSpec((SUBLANES, OUT_TM), lambda i: (0, i))
    return pl.pallas_call(
        _out_proj_kernel,
        grid=(t // OUT_TM,),
        in_specs=[rows(D_MODEL), rows(D_CONV), rows(D_ATT), rows(D_CFM),
                  whole(D_MIX, D_MODEL), whole(1, D_MODEL), whole(ROUTER_ROWS, D_MODEL), whole(ROUTER_ROWS, 1)],
        out_specs=[rows(D_MODEL), rows(D_MODEL), cols, cols, whole(N_EXPERTS, LANES)],
        out_shape=[jax.ShapeDtypeStruct((t, D_MODEL), F32), jax.ShapeDtypeStruct((t, D_MODEL), F32),
                   jax.ShapeDtypeStruct((SUBLANES, t), I32), jax.ShapeDtypeStruct((SUBLANES, t), F32),
                   jax.ShapeDtypeStruct((N_EXPERTS, LANES), F32)],
        scratch_shapes=[pltpu.VMEM((OUT_TM, OUT_TM), BF16), pltpu.VMEM((N_EXPERTS, LANES), F32)],
        compiler_params=_cparams(("arbitrary",)),
        name="out_proj",
    )(x, ya, yb, yc, w_bf, g.reshape(1, D_MODEL), wrt_bf, brt)


def _dispatch_kernel(pad_row_ref, pad_n_ref, nb_ref, dest_ref, h2_ref, xs_hbm, stage, sem, pad_sem):
    i = pl.program_id(0)
    n = pl.num_programs(0)
    slot = lax.rem(i, 2)

    def row_copy(s_idx, r, dst_row, s):
        return pltpu.make_async_copy(stage.at[s_idx, pl.ds(r, 1)], xs_hbm.at[pl.ds(dst_row, 1)], s)

    def tile_wait(s_idx):
        for _ in range(TOPK_IN_GROUP):
            pltpu.make_async_copy(stage.at[s_idx], xs_hbm.at[pl.ds(0, TOK_TM)], sem.at[s_idx]).wait()

    def for_pad_rows(fn):
        def per_expert(e, carry):
            def one(k, c):
                fn(row_copy(0, 0, pad_row_ref[e] + k, pad_sem.at[0]))
                return c
            lax.fori_loop(0, pad_n_ref[e], one, 0)
            return carry
        lax.fori_loop(0, N_EXPERTS, per_expert, 0)

    def for_tail_blocks(fn):
        def one(b, c):
            start = pl.multiple_of(b * FFN_TM, FFN_TM)
            fn(pltpu.make_async_copy(stage.at[0], xs_hbm.at[pl.ds(start, FFN_TM)], pad_sem.at[0]))
            return c
        lax.fori_loop(nb_ref[0], xs_hbm.shape[0] // FFN_TM, one, 0)

    @pl.when(i == 1)
    def _():
        for_pad_rows(lambda cp: cp.wait())
        for_tail_blocks(lambda cp: cp.wait())

    stage[slot] = h2_ref[...]

    @pl.when(i == 0)
    def _():
        for_pad_rows(lambda cp: cp.start())
        for_tail_blocks(lambda cp: cp.start())

    def issue(r, carry):
        row_copy(slot, r, dest_ref[0, 0, r], sem.at[slot]).start()
        row_copy(slot, r, dest_ref[0, 0, TOK_TM + r], sem.at[slot]).start()
        return carry

    lax.fori_loop(0, TOK_TM, issue, 0, unroll=8)

    @pl.when(i > 0)
    def _():
        tile_wait(1 - slot)

    @pl.when(i == n - 1)
    def _():
        tile_wait(slot)


def _dispatch(h2, dest3, pad_row, pad_n, n_used, n_rows):
    nt = dest3.shape[0]
    assert nt >= 2 and TOK_TM == FFN_TM
    return pl.pallas_call(
        _dispatch_kernel,
        grid_spec=pltpu.PrefetchScalarGridSpec(
            num_scalar_prefetch=3,
            grid=(nt,),
            in_specs=[
                pl.BlockSpec((1, 1, 2 * TOK_TM), lambda i, a, b, c: (i, 0, 0), memory_space=pltpu.SMEM),
                pl.BlockSpec((TOK_TM, D_MODEL), lambda i, a, b, c: (i, 0)),
            ],
            out_specs=pl.BlockSpec(memory_space=pl.ANY),
            scratch_shapes=[pltpu.VMEM((2, TOK_TM, D_MODEL), F32),
                            pltpu.SemaphoreType.DMA((2,)), pltpu.SemaphoreType.DMA((1,))],
        ),
        out_shape=jax.ShapeDtypeStruct((n_rows, D_MODEL), F32),
        compiler_params=_cparams(("arbitrary",)),
        name="dispatch",
    )(pad_row, pad_n, n_used, dest3, h2)


def _ffn_kernel(be_ref, nb_ref, xs_ref, wg_ref, wu_ref, wd_ref, y_ref, wgb, wub, wdb):
    i = pl.program_id(0)
    nb = nb_ref[0]

    @pl.when(i < nb)
    def _():
        prev = be_ref[jnp.maximum(i - 1, 0)]

        @pl.when((i == 0) | (be_ref[i] != prev))
        def _():
            wgb[...] = wg_ref[...].astype(BF16)
            wub[...] = wu_ref[...].astype(BF16)
            wdb[...] = wd_ref[...].astype(BF16)

        xb = xs_ref[...].astype(BF16)
        g = jnp.dot(xb, wgb[...], preferred_element_type=F32)
        up = jnp.dot(xb, wub[...], preferred_element_type=F32)
        hid = (g * jax.nn.sigmoid(g) * up).astype(BF16)
        y_ref[...] = jnp.dot(hid, wdb[...], preferred_element_type=F32)

    @pl.when(i >= nb)
    def _():
        y_ref[...] = jnp.zeros(y_ref.shape, y_ref.dtype)


def _ffn(xs, blk_expert, n_used, w_gate, w_up, w_down, layer):
    n_rows = xs.shape[0]
    n_blocks = n_rows // FFN_TM
    wspec = lambda a, b: pl.BlockSpec((None, None, a, b), lambda i, be, nb: (layer, be[i], 0, 0))
    return pl.pallas_call(
        _ffn_kernel,
        grid_spec=pltpu.PrefetchScalarGridSpec(
            num_scalar_prefetch=2,
            grid=(n_blocks,),
            in_specs=[
                pl.BlockSpec((FFN_TM, D_MODEL), lambda i, be, nb: (jnp.minimum(i, nb[0] - 1), 0)),
                wspec(D_MODEL, D_EXPERT), wspec(D_MODEL, D_EXPERT), wspec(D_EXPERT, D_MODEL),
            ],
            out_specs=pl.BlockSpec((FFN_TM, D_MODEL), lambda i, be, nb: (i, 0)),
            scratch_shapes=[
                pltpu.VMEM((D_MODEL, D_EXPERT), BF16),
                pltpu.VMEM((D_MODEL, D_EXPERT), BF16),
                pltpu.VMEM((D_EXPERT, D_MODEL), BF16),
            ],
        ),
        out_shape=jax.ShapeDtypeStruct((n_rows, D_MODEL), F32),
        compiler_params=_cparams(("arbitrary",)),
        name="ffn",
    )(blk_expert, n_used, xs, w_gate, w_up, w_down)


def _gather_rows(idx_ref, n, src_hbm, dst, sem):
    def body(r, carry):
        pltpu.make_async_copy(src_hbm.at[pl.ds(idx_ref[0, 0, r], 1)], dst.at[pl.ds(r, 1)], sem).start()
        return carry
    lax.fori_loop(0, n, body, 0, unroll=8)


def _wait_rows(n, src_hbm, dst, sem):
    pltpu.make_async_copy(src_hbm.at[pl.ds(0, n)], dst, sem).wait()


def _combine_kernel(pos_ref, posn_ref, x1_ref, gt_ref, y_hbm, g_ref, o_ref, ybuf, sem, *, final_norm):
    i = pl.program_id(0)
    n = pl.num_programs(0)
    slot = lax.rem(i, 2)
    tm = TOK_TM

    @pl.when(i == 0)
    def _():
        _gather_rows(pos_ref, 2 * tm, y_hbm, ybuf.at[0], sem.at[0])

    @pl.when(i + 1 < n)
    def _():
        _gather_rows(posn_ref, 2 * tm, y_hbm, ybuf.at[1 - slot], sem.at[1 - slot])

    _wait_rows(2 * tm, y_hbm, ybuf.at[slot], sem.at[slot])
    out = x1_ref[...] + (ybuf[slot, 0:tm, :] * gt_ref[:, 0:1] + ybuf[slot, tm:2 * tm, :] * gt_ref[:, 1:2])
    if final_norm:
        ms = jnp.mean(out * out, axis=-1, keepdims=True)
        out = out * lax.rsqrt(ms + EPS) * g_ref[...]
    o_ref[...] = out


def _combine(x1, y, dest3, gates_t, g, final_norm):
    t = x1.shape[0]
    nt = t // TOK_TM
    return pl.pallas_call(
        functools.partial(_combine_kernel, final_norm=final_norm),
        grid=(nt,),
        in_specs=[
            pl.BlockSpec((1, 1, 2 * TOK_TM), lambda i: (i, 0, 0), memory_space=pltpu.SMEM),
            pl.BlockSpec((1, 1, 2 * TOK_TM), lambda i: (jnp.minimum(i + 1, nt - 1), 0, 0),
                         memory_space=pltpu.SMEM),
            pl.BlockSpec((TOK_TM, D_MODEL), lambda i: (i, 0)),
            pl.BlockSpec((TOK_TM, TOPK_IN_GROUP), lambda i: (i, 0)),
            pl.BlockSpec(memory_space=pl.ANY),
            pl.BlockSpec((1, D_MODEL), lambda i: (0, 0)),
        ],
        out_specs=pl.BlockSpec((TOK_TM, D_MODEL), lambda i: (i, 0)),
        out_shape=jax.ShapeDtypeStruct((t, D_MODEL), F32),
        scratch_shapes=[pltpu.VMEM((2, 2 * TOK_TM, D_MODEL), F32), pltpu.SemaphoreType.DMA((2,))],
        compiler_params=_cparams(("arbitrary",)),
        name="combine",
    )(dest3, dest3, x1, gates_t, y, g.reshape(1, D_MODEL))


def _row_tables(ri, counts_f, t):
    counts = counts_f[:, 0].astype(I32)
    padded = (counts + FFN_TM - 1) // FFN_TM * FFN_TM
    end_p = jnp.cumsum(padded)
    start_p = end_p - padded
    n_rows = t * TOPK_IN_GROUP + N_EXPERTS * FFN_TM
    blk_start = jnp.arange(n_rows // FFN_TM, dtype=I32) * FFN_TM
    blk_expert = jnp.minimum(jnp.sum((end_p[None, :] <= blk_start[:, None]).astype(I32), axis=1), N_EXPERTS - 1)
    n_used = (end_p[-1] // FFN_TM).astype(I32).reshape(1)
    expert = ri[0:TOPK_IN_GROUP]
    rank = ri[TOPK_IN_GROUP:2 * TOPK_IN_GROUP]
    first = jnp.sum(jnp.where(expert[:, :, None] == jnp.arange(N_EXPERTS, dtype=I32), start_p, 0), axis=-1)
    dest = rank + first
    nt = t // TOK_TM
    dest3 = dest.reshape(TOPK_IN_GROUP, nt, TOK_TM).transpose(1, 0, 2).reshape(nt, 1, TOPK_IN_GROUP * TOK_TM)
    return blk_expert, n_used, dest3, (start_p + counts).astype(I32), (padded - counts).astype(I32), n_rows


def _router_weights(rg_w, rg_b, re_w, re_b):
    wrt = jnp.zeros((ROUTER_ROWS, D_MODEL), F32)
    wrt = wrt.at[0:N_GROUPS].set(rg_w.T).at[ROUTER_E0:ROUTER_E0 + N_EXPERTS].set(re_w.T)
    brt = jnp.zeros((ROUTER_ROWS,), F32)
    brt = brt.at[0:N_GROUPS].set(rg_b).at[ROUTER_E0:ROUTER_E0 + N_EXPERTS].set(re_b)
    return wrt.astype(BF16), brt.reshape(ROUTER_ROWS, 1)


def kernel(x, norm1_g, w_in, conv_a_w, conv_c_w, conv_c_b, ln_c_g, ln_c_b, w_out, norm2_g,
           router_g_w, router_g_b, router_e_w, router_e_b, w_gate, w_up, w_down, final_g):
    batch, seq, d = x.shape
    t = batch * seq
    depth = w_in.shape[0]
    xt = x.reshape(t, d)
    slopes = jnp.exp2(-8.0 * jnp.arange(1, N_HEADS + 1, dtype=F32) / N_HEADS)
    for l in range(depth):
        u = _in_proj(xt, norm1_g[l], w_in[l].astype(BF16))
        ya, yc = _mixers(u, conv_a_w[l], conv_c_w[l], conv_c_b[l], ln_c_g[l], ln_c_b[l], batch, seq)
        yb = _moba(u, slopes, batch, seq)
        wrt, brt = _router_weights(router_g_w[l], router_g_b[l], router_e_w[l], router_e_b[l])
        x1, h2p, ri, rf, counts = _out_proj(xt, ya, yb, yc, w_out[l].astype(BF16), norm2_g[l], wrt, brt)
        blk_expert, n_used, dest3, pad_row, pad_n, n_rows = _row_tables(ri, counts, t)
        xs = _dispatch(h2p, dest3, pad_row, pad_n, n_used, n_rows)
        y = _ffn(xs, blk_expert, n_used, w_gate, w_up, w_down, l)
        xt = _combine(x1, y, dest3, rf[0:TOPK_IN_GROUP].T, final_g, final_norm=(l == depth - 1))
    return xt.reshape(batch, seq, d)
```

```python
import functools

import jax
import jax.numpy as jnp
from jax import lax
from jax.experimental import pallas as pl
from jax.experimental.pallas import tpu as pltpu

D_MODEL = 2048
D_CONV = 512
D_ATT = 1024
N_HEADS = 8
HEAD_DIM = 128
D_CFM = 512
D_MIX = D_CONV + D_ATT + D_CFM
D_IN = 3 * D_CONV + 3 * D_ATT + 2 * D_CFM
SHORT_CONV_W = 3
CFM_CONV_W = 31
MOBA_BLOCK = 256
MOBA_TOPK = 3
N_GROUPS = 4
EXPERTS_PER_GROUP = 8
N_EXPERTS = N_GROUPS * EXPERTS_PER_GROUP
TOPK_IN_GROUP = 2
D_EXPERT = 512
EPS = 1e-6
NEG = -1e30

LANES = 128
SUBLANES = 8
BF16_ROWS = 16
VMEM_LIMIT = 56 * 1024 * 1024

IN_TM = 1024
IN_TN = 512
MIX_TS = 256
HALO_A = 8
HALO_C = 32
MOBA_AUG = 2 * HEAD_DIM
MOBA_TILES_PER_STEP = 4
OUT_TM = 256
ROUTER_ROWS = LANES
ROUTER_E0 = SUBLANES
FFN_TM = 256
TOK_TM = 256

BF16 = jnp.bfloat16
F32 = jnp.float32
I32 = jnp.int32


def _cparams(sem):
    return pltpu.CompilerParams(dimension_semantics=sem, vmem_limit_bytes=VMEM_LIMIT)


def _in_proj_kernel(x_ref, g_ref, w_ref, o_ref, hn_ref):
    j = pl.program_id(1)

    @pl.when(j == 0)
    def _():
        x = x_ref[...]
        ms = jnp.mean(x * x, axis=-1, keepdims=True)
        hn_ref[...] = (x * lax.rsqrt(ms + EPS) * g_ref[...]).astype(BF16)

    q_lo = (3 * D_CONV) // IN_TN
    q_hi = (3 * D_CONV + D_ATT) // IN_TN
    col_scale = jnp.where((j >= q_lo) & (j < q_hi), HEAD_DIM ** -0.5, 1.0).astype(F32)
    acc = jnp.dot(hn_ref[...], w_ref[...], preferred_element_type=F32)
    o_ref[...] = (acc * col_scale).astype(o_ref.dtype)


def _in_proj(x, g, w_bf):
    t = x.shape[0]
    return pl.pallas_call(
        _in_proj_kernel,
        grid=(t // IN_TM, D_IN // IN_TN),
        in_specs=[
            pl.BlockSpec((IN_TM, D_MODEL), lambda i, j: (i, 0)),
            pl.BlockSpec((1, D_MODEL), lambda i, j: (0, 0)),
            pl.BlockSpec((D_MODEL, IN_TN), lambda i, j: (0, j)),
        ],
        out_specs=pl.BlockSpec((IN_TM, IN_TN), lambda i, j: (i, j)),
        out_shape=jax.ShapeDtypeStruct((t, D_IN), BF16),
        scratch_shapes=[pltpu.VMEM((IN_TM, D_MODEL), BF16)],
        compiler_params=_cparams(("arbitrary", "arbitrary")),
        name="in_proj",
    )(x, g.reshape(1, D_MODEL), w_bf)


def _mix_kernel(bg_ref, cg_ref, xv_ref, a_ref, gt_ref, wa_ref, wc_ref, bc_ref, lg_ref, lb_ref,
                ya_ref, yc_ref, bufa, bufc, shc):
    s = pl.program_id(1)
    ts = MIX_TS

    @pl.when(s == 0)
    def _():
        bufa[0:HALO_A, :] = jnp.zeros((HALO_A, D_CONV), F32)
        bufc[0:HALO_C, :] = jnp.zeros((HALO_C, D_CFM), F32)

    @pl.when(s > 0)
    def _():
        bufa[0:HALO_A, :] = bufa[ts:ts + HALO_A, :]
        bufc[0:HALO_C, :] = bufc[ts:ts + HALO_C, :]

    bufa[HALO_A:HALO_A + ts, :] = cg_ref[...].astype(F32) * xv_ref[...].astype(F32)
    acc = jnp.zeros((ts, D_CONV), F32)
    for j in range(SHORT_CONV_W):
        off = HALO_A - (SHORT_CONV_W - 1) + j
        acc = acc + wa_ref[j:j + 1, :] * bufa[off:off + ts, :]
    ya_ref[...] = (bg_ref[...].astype(F32) * acc).astype(ya_ref.dtype)

    a = a_ref[...].astype(F32)
    gate = gt_ref[...].astype(F32)
    bufc[HALO_C:HALO_C + ts, :] = a * jax.nn.sigmoid(gate)
    span = ts + HALO_C - SUBLANES
    for r in range(1, SUBLANES):
        shc[r - 1, :, :] = bufc[r:r + span, :]
    acc = jnp.zeros((ts, D_CFM), F32) + bc_ref[...]
    for j in range(CFM_CONV_W):
        off = HALO_C - (CFM_CONV_W - 1) + j
        a, r = divmod(off, SUBLANES)
        win = bufc[off:off + ts, :] if r == 0 else shc[r - 1, a * SUBLANES:a * SUBLANES + ts, :]
        acc = acc + wc_ref[j:j + 1, :] * win
    mu = jnp.mean(acc, axis=-1, keepdims=True)
    cen = acc - mu
    var = jnp.mean(cen * cen, axis=-1, keepdims=True)
    hn = cen * lax.rsqrt(var + EPS) * lg_ref[...] + lb_ref[...]
    yc_ref[...] = (hn * jax.nn.sigmoid(hn)).astype(yc_ref.dtype)


def _mixers(u, wa, wc, bc, lg, lb, batch, seq):
    t = u.shape[0]
    nst = seq // MIX_TS
    row = lambda b, s: b * nst + s
    ublk = lambda c: pl.BlockSpec((MIX_TS, D_CONV), lambda b, s, c=c: (row(b, s), c))
    small = lambda r: pl.BlockSpec((r, D_CONV), lambda b, s: (0, 0))
    c0 = (3 * D_CONV + 3 * D_ATT) // D_CFM
    return pl.pallas_call(
        _mix_kernel,
        grid=(batch, nst),
        in_specs=[ublk(0), ublk(1), ublk(2), ublk(c0), ublk(c0 + 1),
                  small(SHORT_CONV_W), small(CFM_CONV_W), small(1), small(1), small(1)],
        out_specs=[pl.BlockSpec((MIX_TS, D_CONV), lambda b, s: (row(b, s), 0)),
                   pl.BlockSpec((MIX_TS, D_CFM), lambda b, s: (row(b, s), 0))],
        out_shape=[jax.ShapeDtypeStruct((t, D_CONV), BF16), jax.ShapeDtypeStruct((t, D_CFM), BF16)],
        scratch_shapes=[pltpu.VMEM((HALO_A + MIX_TS, D_CONV), F32),
                        pltpu.VMEM((HALO_C + MIX_TS, D_CFM), F32),
                        pltpu.VMEM((SUBLANES - 1, MIX_TS + HALO_C - SUBLANES, D_CFM), F32)],
        compiler_params=_cparams(("arbitrary", "arbitrary")),
        name="mixers",
    )(u, u, u, u, u, wa, wc, bc.reshape(1, D_CFM), lg.reshape(1, D_CFM), lb.reshape(1, D_CFM))


_NT = (((1,), (1,)), ((), ()))

F_MASK = HEAD_DIM
F_BIAS = HEAD_DIM + BF16_ROWS


def _moba_prepare(slope, q_ref, k_ref, v_ref, kaug, vt, qaug, ident):
    blk = MOBA_BLOCK
    seq = k_ref.shape[0]
    nb = seq // blk
    ri = lax.broadcasted_iota(I32, (blk, blk), 0)
    ci = lax.broadcasted_iota(I32, (blk, blk), 1)
    ident[...] = jnp.where(ri == ci, 1.0, 0.0).astype(BF16)
    eye_hd = ident[0:HEAD_DIM, 0:HEAD_DIM]

    kf = k_ref[...].astype(F32).reshape(nb, blk, HEAD_DIM)
    km = jnp.sum(kf, axis=1) * (1.0 / blk)
    km_hi = km.astype(BF16)
    km_lo = (km - km_hi.astype(F32)).astype(BF16)

    kaug[:, 0:HEAD_DIM] = k_ref[...]
    lane = lax.broadcasted_iota(I32, (blk, LANES), 1)
    cc = lax.broadcasted_iota(I32, (blk, LANES), 0).astype(F32)
    fb = F_BIAS - HEAD_DIM

    common = jnp.where((lane == fb) | (lane == fb + 1), 1.0, jnp.where(lane == fb + 2, slope * cc, 0.0))
    for j in range(nb):
        feat = jnp.where(lane == j, 1.0, jnp.where(lane == fb + 3, slope * float(j * blk), common))
        kaug[j * blk:(j + 1) * blk, HEAD_DIM:MOBA_AUG] = feat.astype(BF16)
        vt[j] = lax.dot_general(eye_hd, v_ref[j * blk:(j + 1) * blk, :], _NT,
                                preferred_element_type=F32).astype(BF16)

    q = q_ref[...]
    qaug[0:HEAD_DIM, :] = lax.dot_general(eye_hd, q, _NT, preferred_element_type=F32).astype(BF16)

    gate = (lax.dot_general(km_hi, q, _NT, preferred_element_type=F32)
            + lax.dot_general(km_lo, q, _NT, preferred_element_type=F32))
    sub = lax.broadcasted_iota(I32, (nb, seq), 0)
    tpos = lax.broadcasted_iota(I32, (nb, seq), 1)
    qblk = lax.shift_right_logical(tpos, blk.bit_length() - 1)
    rank = jnp.zeros((nb, seq), F32)
    for jp in range(nb):
        row = gate[jp:jp + 1, :]
        beats = (row > gate) | ((row == gate) & (sub > jp))
        rank = rank + jnp.where(beats & (qblk > jp), 1.0, 0.0)
    allowed = ((sub < qblk) & (rank < MOBA_TOPK)) | (sub == qblk)
    qaug[F_MASK:F_MASK + BF16_ROWS, :] = jnp.where(allowed, 0.0, NEG).astype(BF16)

    rq = (tpos & (blk - 1)).astype(F32)
    bias = jnp.where(sub == 0, -slope * rq, 0.0)
    bias = jnp.where(sub == 1, -slope * (qblk * blk).astype(F32), bias)
    bias = jnp.where((sub == 2) | (sub == 3), 1.0, bias)
    qaug[F_BIAS:F_BIAS + BF16_ROWS, :] = bias.astype(BF16)
    qaug[F_BIAS + BF16_ROWS:MOBA_AUG, :] = jnp.zeros((MOBA_AUG - F_BIAS - BF16_ROWS, seq), BF16)


def _moba_tile(n, kaug, vt, qaug, ident, o_ref):
    blk = MOBA_BLOCK
    qa = qaug[:, n * blk:(n + 1) * blk]
    kc = lax.broadcasted_iota(I32, (blk, blk), 0)
    qr = lax.broadcasted_iota(I32, (blk, blk), 1)
    ss = [jnp.dot(kaug[j * blk:(j + 1) * blk, :], qa, preferred_element_type=F32) for j in range(n + 1)]
    ss[n] = jnp.where(kc <= qr, ss[n], NEG)
    m = jnp.max(ss[0], axis=0, keepdims=True)
    for j in range(1, n + 1):
        m = jnp.maximum(m, jnp.max(ss[j], axis=0, keepdims=True))
    l = jnp.zeros((1, blk), F32)
    acc = jnp.zeros((HEAD_DIM, blk), F32)
    for j in range(n + 1):
        p = jnp.exp(ss[j] - m)
        l = l + jnp.sum(p, axis=0, keepdims=True)
        acc = acc + jnp.dot(vt[j], p.astype(BF16), preferred_element_type=F32)
    out_t = (acc / l).astype(BF16)
    o_ref[n * blk:(n + 1) * blk, :] = lax.dot_general(ident[...], out_t, _NT,
                                                      preferred_element_type=F32).astype(o_ref.dtype)


def _moba_kernel(slopes_ref, q_ref, k_ref, v_ref, o_ref, kaug, vt, qaug, ident):
    step = pl.program_id(2)
    nq = k_ref.shape[0] // MOBA_BLOCK

    @pl.when(step == 0)
    def _():
        _moba_prepare(slopes_ref[pl.program_id(1)], q_ref, k_ref, v_ref, kaug, vt, qaug, ident)

    for s in range(nq // MOBA_TILES_PER_STEP):
        @pl.when(step == s)
        def _(s=s):
            for n in _moba_step_tiles(s, nq):
                _moba_tile(n, kaug, vt, qaug, ident, o_ref)


def _moba_step_tiles(s, nq):
    pairs = MOBA_TILES_PER_STEP // 2
    out = []
    for k in range(s * pairs, (s + 1) * pairs):
        out += [k, nq - 1 - k]
    return out


def _moba(u, slopes, batch, seq):
    t = u.shape[0]
    nq = seq // MOBA_BLOCK
    assert nq == BF16_ROWS
    qc = (3 * D_CONV) // HEAD_DIM
    kc = qc + N_HEADS
    vc = kc + N_HEADS
    whole = lambda c0: pl.BlockSpec((seq, HEAD_DIM), lambda b, h, i, s: (b, c0 + h))
    return pl.pallas_call(
        _moba_kernel,
        grid_spec=pltpu.PrefetchScalarGridSpec(
            num_scalar_prefetch=1,
            grid=(batch, N_HEADS, nq // MOBA_TILES_PER_STEP),
            in_specs=[whole(qc), whole(kc), whole(vc)],
            out_specs=pl.BlockSpec((seq, HEAD_DIM), lambda b, h, i, s: (b, h)),
            scratch_shapes=[
                pltpu.VMEM((seq, MOBA_AUG), BF16),
                pltpu.VMEM((nq, HEAD_DIM, MOBA_BLOCK), BF16),
                pltpu.VMEM((MOBA_AUG, seq), BF16),
                pltpu.VMEM((MOBA_BLOCK, MOBA_BLOCK), BF16),
            ],
        ),
        out_shape=jax.ShapeDtypeStruct((t, D_ATT), BF16),
        compiler_params=_cparams(("arbitrary", "arbitrary", "arbitrary")),
        name="moba",
    )(slopes, u, u, u)


def _first_max(v, rows):
    top = jnp.max(v, axis=0, keepdims=True)
    idx = jnp.min(jnp.where(v == top, rows, float(v.shape[0])), axis=0, keepdims=True)
    return top, idx


def _router(lg, tri_ref, carry_ref):
    tm = lg.shape[1]
    rows = lax.broadcasted_iota(I32, (SUBLANES, tm), 0).astype(F32)
    grp_lg = jnp.where(rows < N_GROUPS, lg[0:SUBLANES, :], NEG)
    ex = jnp.exp(grp_lg - jnp.max(grp_lg, axis=0, keepdims=True))
    pg = ex / jnp.sum(ex, axis=0, keepdims=True)
    p_grp, grp = _first_max(pg, rows)
    le = jnp.zeros((EXPERTS_PER_GROUP, tm), F32)
    for g in range(N_GROUPS):
        r0 = ROUTER_E0 + g * EXPERTS_PER_GROUP
        le = jnp.where(grp == g, lg[r0:r0 + EXPERTS_PER_GROUP, :], le)
    ex = jnp.exp(le - jnp.max(le, axis=0, keepdims=True))
    q = ex / jnp.sum(ex, axis=0, keepdims=True)
    q1, i1 = _first_max(q, rows)
    q2, i2 = _first_max(jnp.where(rows == i1, -1.0, q), rows)
    den = q1 + q2
    g0 = p_grp * q1 / den
    g1 = p_grp * q2 / den
    e0 = grp * EXPERTS_PER_GROUP + i1
    e1 = grp * EXPERTS_PER_GROUP + i2

    erow = lax.broadcasted_iota(I32, (N_EXPERTS, tm), 0).astype(F32)
    is0 = erow == e0
    is1 = erow == e1
    onehot = jnp.where(is0 | is1, 1.0, 0.0)
    before = jnp.dot(onehot.astype(BF16), tri_ref[...], preferred_element_type=F32) + carry_ref[:, 0:1]
    r0 = jnp.sum(jnp.where(is0, before, 0.0), axis=0, keepdims=True)
    r1 = jnp.sum(jnp.where(is1, before, 0.0), axis=0, keepdims=True)
    carry_ref[...] = carry_ref[...] + jnp.sum(onehot, axis=1, keepdims=True)
    return e0, e1, g0, g1, r0, r1


def _out_proj_kernel(x_ref, ya_ref, yb_ref, yc_ref, w_ref, g_ref, wrt_ref, brt_ref,
                     x1_ref, h2_ref, ri_ref, rf_ref, cnt_ref, tri_ref, carry_ref):
    tm = OUT_TM

    @pl.when(pl.program_id(0) == 0)
    def _():
        a = lax.broadcasted_iota(I32, (tm, tm), 0)
        b = lax.broadcasted_iota(I32, (tm, tm), 1)
        tri_ref[...] = jnp.where(a < b, 1.0, 0.0).astype(BF16)
        carry_ref[...] = jnp.zeros(carry_ref.shape, F32)

    acc = x_ref[...]
    acc = acc + jnp.dot(ya_ref[...], w_ref[0:D_CONV, :], preferred_element_type=F32)
    acc = acc + jnp.dot(yb_ref[...], w_ref[D_CONV:D_CONV + D_ATT, :], preferred_element_type=F32)
    acc = acc + jnp.dot(yc_ref[...], w_ref[D_CONV + D_ATT:D_MIX, :], preferred_element_type=F32)
    x1_ref[...] = acc
    ms = jnp.mean(acc * acc, axis=-1, keepdims=True)
    h2 = acc * lax.rsqrt(ms + EPS) * g_ref[...]
    h2_ref[...] = h2

    lg = lax.dot_general(wrt_ref[...], h2.astype(BF16), _NT, preferred_element_type=F32) + brt_ref[...]
    e0, e1, g0, g1, r0, r1 = _router(lg, tri_ref, carry_ref)
    row = lax.broadcasted_iota(I32, (SUBLANES, tm), 0)
    ints = jnp.where(row == 0, e0, jnp.where(row == 1, e1, jnp.where(row == 2, r0, jnp.where(row == 3, r1, 0.0))))
    ri_ref[...] = ints.astype(I32)
    rf_ref[...] = jnp.where(row == 0, g0, jnp.where(row == 1, g1, 0.0))
    cnt_ref[...] = carry_ref[...]


def _out_proj(x, ya, yb, yc, w_bf, g, wrt_bf, brt):
    t = x.shape[0]
    rows = lambda n: pl.BlockSpec((OUT_TM, n), lambda i: (i, 0))
    whole = lambda a, b: pl.BlockSpec((a, b), lambda i: (0, 0))
    cols = pl.BlockSpec((SUBLANES, OUT_TM), lambda i: (0, i))
    return pl.pallas_call(
        _out_proj_kernel,
        grid=(t // OUT_TM,),
        in_specs=[rows(D_MODEL), rows(D_CONV), rows(D_ATT), rows(D_CFM),
                  whole(D_MIX, D_MODEL), whole(1, D_MODEL), whole(ROUTER_ROWS, D_MODEL), whole(ROUTER_ROWS, 1)],
        out_specs=[rows(D_MODEL), rows(D_MODEL), cols, cols, whole(N_EXPERTS, LANES)],
        out_shape=[jax.ShapeDtypeStruct((t, D_MODEL), F32), jax.ShapeDtypeStruct((t, D_MODEL), F32),
                   jax.ShapeDtypeStruct((SUBLANES, t), I32), jax.ShapeDtypeStruct((SUBLANES, t), F32),
                   jax.ShapeDtypeStruct((N_EXPERTS, LANES), F32)],
        scratch_shapes=[pltpu.VMEM((OUT_TM, OUT_TM), BF16), pltpu.VMEM((N_EXPERTS, LANES), F32)],
        compiler_params=_cparams(("arbitrary",)),
        name="out_proj",
    )(x, ya, yb, yc, w_bf, g.reshape(1, D_MODEL), wrt_bf, brt)


def _dispatch_kernel(pad_row_ref, pad_n_ref, nb_ref, dest_ref, h2_ref, xs_hbm, stage, sem, pad_sem):
    i = pl.program_id(0)
    n = pl.num_programs(0)
    slot = lax.rem(i, 2)

    def row_copy(s_idx, r, dst_row, s):
        return pltpu.make_async_copy(stage.at[s_idx, pl.ds(r, 1)], xs_hbm.at[pl.ds(dst_row, 1)], s)

    def tile_wait(s_idx):
        for _ in range(TOPK_IN_GROUP):
            pltpu.make_async_copy(stage.at[s_idx], xs_hbm.at[pl.ds(0, TOK_TM)], sem.at[s_idx]).wait()

    def for_pad_rows(fn):
        def per_expert(e, carry):
            def one(k, c):
                fn(row_copy(0, 0, pad_row_ref[e] + k, pad_sem.at[0]))
                return c
            lax.fori_loop(0, pad_n_ref[e], one, 0)
            return carry
        lax.fori_loop(0, N_EXPERTS, per_expert, 0)

    def for_tail_blocks(fn):
        def one(b, c):
            start = pl.multiple_of(b * FFN_TM, FFN_TM)
            fn(pltpu.make_async_copy(stage.at[0], xs_hbm.at[pl.ds(start, FFN_TM)], pad_sem.at[0]))
            return c
        lax.fori_loop(nb_ref[0], xs_hbm.shape[0] // FFN_TM, one, 0)

    @pl.when(i == 1)
    def _():
        for_pad_rows(lambda cp: cp.wait())
        for_tail_blocks(lambda cp: cp.wait())

    stage[slot] = h2_ref[...]

    @pl.when(i == 0)
    def _():
        for_pad_rows(lambda cp: cp.start())
        for_tail_blocks(lambda cp: cp.start())

    def issue(r, carry):
        row_copy(slot, r, dest_ref[0, 0, r], sem.at[slot]).start(priority=0)
        row_copy(slot, r, dest_ref[0, 0, TOK_TM + r], sem.at[slot]).start(priority=1)
        return carry

    lax.fori_loop(0, TOK_TM, issue, 0, unroll=8)

    @pl.when(i > 0)
    def _():
        tile_wait(1 - slot)

    @pl.when(i == n - 1)
    def _():
        tile_wait(slot)


def _dispatch(h2, dest3, pad_row, pad_n, n_used, n_rows):
    nt = dest3.shape[0]
    assert nt >= 2 and TOK_TM == FFN_TM
    return pl.pallas_call(
        _dispatch_kernel,
        grid_spec=pltpu.PrefetchScalarGridSpec(
            num_scalar_prefetch=3,
            grid=(nt,),
            in_specs=[
                pl.BlockSpec((1, 1, 2 * TOK_TM), lambda i, a, b, c: (i, 0, 0), memory_space=pltpu.SMEM),
                pl.BlockSpec((TOK_TM, D_MODEL), lambda i, a, b, c: (i, 0)),
            ],
            out_specs=pl.BlockSpec(memory_space=pl.ANY),
            scratch_shapes=[pltpu.VMEM((2, TOK_TM, D_MODEL), F32),
                            pltpu.SemaphoreType.DMA((2,)), pltpu.SemaphoreType.DMA((1,))],
        ),
        out_shape=jax.ShapeDtypeStruct((n_rows, D_MODEL), F32),
        compiler_params=_cparams(("arbitrary",)),
        name="dispatch",
    )(pad_row, pad_n, n_used, dest3, h2)


def _ffn_kernel(be_ref, nb_ref, xs_ref, wg_ref, wu_ref, wd_ref, y_ref, wgb, wub, wdb):
    i = pl.program_id(0)
    nb = nb_ref[0]

    @pl.when(i < nb)
    def _():
        prev = be_ref[jnp.maximum(i - 1, 0)]

        @pl.when((i == 0) | (be_ref[i] != prev))
        def _():
            wgb[...] = wg_ref[...].astype(BF16)
            wub[...] = wu_ref[...].astype(BF16)
            wdb[...] = wd_ref[...].astype(BF16)

        xb = xs_ref[...].astype(BF16)
        g = jnp.dot(xb, wgb[...], preferred_element_type=F32)
        up = jnp.dot(xb, wub[...], preferred_element_type=F32)
        hid = (g * jax.nn.sigmoid(g) * up).astype(BF16)
        y_ref[...] = jnp.dot(hid, wdb[...], preferred_element_type=F32)

    @pl.when(i >= nb)
    def _():
        y_ref[...] = jnp.zeros(y_ref.shape, y_ref.dtype)


def _ffn(xs, blk_expert, n_used, w_gate, w_up, w_down, layer):
    n_rows = xs.shape[0]
    n_blocks = n_rows // FFN_TM
    wspec = lambda a, b: pl.BlockSpec((None, None, a, b), lambda i, be, nb: (layer, be[i], 0, 0))
    return pl.pallas_call(
        _ffn_kernel,
        grid_spec=pltpu.PrefetchScalarGridSpec(
            num_scalar_prefetch=2,
            grid=(n_blocks,),
            in_specs=[
                pl.BlockSpec((FFN_TM, D_MODEL), lambda i, be, nb: (jnp.minimum(i, nb[0] - 1), 0)),
                wspec(D_MODEL, D_EXPERT), wspec(D_MODEL, D_EXPERT), wspec(D_EXPERT, D_MODEL),
            ],
            out_specs=pl.BlockSpec((FFN_TM, D_MODEL), lambda i, be, nb: (i, 0)),
            scratch_shapes=[
                pltpu.VMEM((D_MODEL, D_EXPERT), BF16),
                pltpu.VMEM((D_MODEL, D_EXPERT), BF16),
                pltpu.VMEM((D_EXPERT, D_MODEL), BF16),
            ],
        ),
        out_shape=jax.ShapeDtypeStruct((n_rows, D_MODEL), F32),
        compiler_params=_cparams(("arbitrary",)),
        name="ffn",
    )(blk_expert, n_used, xs, w_gate, w_up, w_down)


def _gather_rows(idx_ref, n, src_hbm, dst, sem):
    def body(k, carry):
        for parity in range(2):
            r = 2 * k + parity
            pltpu.make_async_copy(src_hbm.at[pl.ds(idx_ref[0, 0, r], 1)], dst.at[pl.ds(r, 1)],
                                  sem).start(priority=parity)
        return carry
    lax.fori_loop(0, n // 2, body, 0, unroll=4)


def _wait_rows(n, src_hbm, dst, sem):
    pltpu.make_async_copy(src_hbm.at[pl.ds(0, n)], dst, sem).wait()


def _combine_kernel(pos_ref, posn_ref, x1_ref, gt_ref, y_hbm, g_ref, o_ref, ybuf, sem, *, final_norm):
    i = pl.program_id(0)
    n = pl.num_programs(0)
    slot = lax.rem(i, 2)
    tm = TOK_TM

    @pl.when(i == 0)
    def _():
        _gather_rows(pos_ref, 2 * tm, y_hbm, ybuf.at[0], sem.at[0])

    @pl.when(i + 1 < n)
    def _():
        _gather_rows(posn_ref, 2 * tm, y_hbm, ybuf.at[1 - slot], sem.at[1 - slot])

    _wait_rows(2 * tm, y_hbm, ybuf.at[slot], sem.at[slot])
    out = x1_ref[...] + (ybuf[slot, 0:tm, :] * gt_ref[:, 0:1] + ybuf[slot, tm:2 * tm, :] * gt_ref[:, 1:2])
    if final_norm:
        ms = jnp.mean(out * out, axis=-1, keepdims=True)
        out = out * lax.rsqrt(ms + EPS) * g_ref[...]
    o_ref[...] = out


def _combine(x1, y, dest3, gates_t, g, final_norm):
    t = x1.shape[0]
    nt = t // TOK_TM
    return pl.pallas_call(
        functools.partial(_combine_kernel, final_norm=final_norm),
        grid=(nt,),
        in_specs=[
            pl.BlockSpec((1, 1, 2 * TOK_TM), lambda i: (i, 0, 0), memory_space=pltpu.SMEM),
            pl.BlockSpec((1, 1, 2 * TOK_TM), lambda i: (jnp.minimum(i + 1, nt - 1), 0, 0),
                         memory_space=pltpu.SMEM),
            pl.BlockSpec((TOK_TM, D_MODEL), lambda i: (i, 0)),
            pl.BlockSpec((TOK_TM, TOPK_IN_GROUP), lambda i: (i, 0)),
            pl.BlockSpec(memory_space=pl.ANY),
            pl.BlockSpec((1, D_MODEL), lambda i: (0, 0)),
        ],
        out_specs=pl.BlockSpec((TOK_TM, D_MODEL), lambda i: (i, 0)),
        out_shape=jax.ShapeDtypeStruct((t, D_MODEL), F32),
        scratch_shapes=[pltpu.VMEM((2, 2 * TOK_TM, D_MODEL), F32), pltpu.SemaphoreType.DMA((2,))],
        compiler_params=_cparams(("arbitrary",)),
        name="combine",
    )(dest3, dest3, x1, gates_t, y, g.reshape(1, D_MODEL))


def _row_tables(ri, counts_f, t):
    counts = counts_f[:, 0].astype(I32)
    padded = (counts + FFN_TM - 1) // FFN_TM * FFN_TM
    end_p = jnp.cumsum(padded)
    start_p = end_p - padded
    n_rows = t * TOPK_IN_GROUP + N_EXPERTS * FFN_TM
    blk_start = jnp.arange(n_rows // FFN_TM, dtype=I32) * FFN_TM
    blk_expert = jnp.minimum(jnp.sum((end_p[None, :] <= blk_start[:, None]).astype(I32), axis=1), N_EXPERTS - 1)
    n_used = (end_p[-1] // FFN_TM).astype(I32).reshape(1)
    expert = ri[0:TOPK_IN_GROUP]
    rank = ri[TOPK_IN_GROUP:2 * TOPK_IN_GROUP]
    first = jnp.sum(jnp.where(expert[:, :, None] == jnp.arange(N_EXPERTS, dtype=I32), start_p, 0), axis=-1)
    dest = rank + first
    nt = t // TOK_TM
    dest3 = dest.reshape(TOPK_IN_GROUP, nt, TOK_TM).transpose(1, 0, 2).reshape(nt, 1, TOPK_IN_GROUP * TOK_TM)
    return blk_expert, n_used, dest3, (start_p + counts).astype(I32), (padded - counts).astype(I32), n_rows


def _router_weights(rg_w, rg_b, re_w, re_b):
    wrt = jnp.zeros((ROUTER_ROWS, D_MODEL), F32)
    wrt = wrt.at[0:N_GROUPS].set(rg_w.T).at[ROUTER_E0:ROUTER_E0 + N_EXPERTS].set(re_w.T)
    brt = jnp.zeros((ROUTER_ROWS,), F32)
    brt = brt.at[0:N_GROUPS].set(rg_b).at[ROUTER_E0:ROUTER_E0 + N_EXPERTS].set(re_b)
    return wrt.astype(BF16), brt.reshape(ROUTER_ROWS, 1)


def kernel(x, norm1_g, w_in, conv_a_w, conv_c_w, conv_c_b, ln_c_g, ln_c_b, w_out, norm2_g,
           router_g_w, router_g_b, router_e_w, router_e_b, w_gate, w_up, w_down, final_g):
    batch, seq, d = x.shape
    t = batch * seq
    depth = w_in.shape[0]
    xt = x.reshape(t, d)
    slopes = jnp.exp2(-8.0 * jnp.arange(1, N_HEADS + 1, dtype=F32) / N_HEADS)
    for l in range(depth):
        u = _in_proj(xt, norm1_g[l], w_in[l].astype(BF16))
        ya, yc = _mixers(u, conv_a_w[l], conv_c_w[l], conv_c_b[l], ln_c_g[l], ln_c_b[l], batch, seq)
        yb = _moba(u, slopes, batch, seq)
        wrt, brt = _router_weights(router_g_w[l], router_g_b[l], router_e_w[l], router_e_b[l])
        x1, h2p, ri, rf, counts = _out_proj(xt, ya, yb, yc, w_out[l].astype(BF16), norm2_g[l], wrt, brt)
        blk_expert, n_used, dest3, pad_row, pad_n, n_rows = _row_tables(ri, counts, t)
        xs = _dispatch(h2p, dest3, pad_row, pad_n, n_used, n_rows)
        y = _ffn(xs, blk_expert, n_used, w_gate, w_up, w_down, l)
        xt = _combine(x1, y, dest3, rf[0:TOPK_IN_GROUP].T, final_g, final_norm=(l == depth - 1))
    return xt.reshape(batch, seq, d)
```

```python
import functools

import jax
import jax.numpy as jnp
from jax import lax
from jax.experimental import pallas as pl
from jax.experimental.pallas import tpu as pltpu

D_MODEL = 2048
D_CONV = 512
D_ATT = 1024
N_HEADS = 8
HEAD_DIM = 128
D_CFM = 512
D_MIX = D_CONV + D_ATT + D_CFM
D_IN = 3 * D_CONV + 3 * D_ATT + 2 * D_CFM
SHORT_CONV_W = 3
CFM_CONV_W = 31
MOBA_BLOCK = 256
MOBA_TOPK = 3
N_GROUPS = 4
EXPERTS_PER_GROUP = 8
N_EXPERTS = N_GROUPS * EXPERTS_PER_GROUP
TOPK_IN_GROUP = 2
D_EXPERT = 512
EPS = 1e-6
NEG = -1e30

LANES = 128
SUBLANES = 8
BF16_ROWS = 16
VMEM_LIMIT = 56 * 1024 * 1024

IN_TM = 1024
IN_TN = 512
MIX_TS = 256
HALO_A = 8
HALO_C = 32
MOBA_AUG = 2 * HEAD_DIM
MOBA_TILES_PER_STEP = 8
OUT_TM = 256
ROUTER_ROWS = LANES
ROUTER_E0 = SUBLANES
FFN_TM = 256
TOK_TM = 256

BF16 = jnp.bfloat16
F32 = jnp.float32
I32 = jnp.int32


def _cparams(sem):
    return pltpu.CompilerParams(dimension_semantics=sem, vmem_limit_bytes=VMEM_LIMIT)


def _in_proj_kernel(x_ref, g_ref, w_ref, o_ref, hn_ref):
    j = pl.program_id(1)

    @pl.when(j == 0)
    def _():
        x = x_ref[...]
        ms = jnp.mean(x * x, axis=-1, keepdims=True)
        hn_ref[...] = (x * lax.rsqrt(ms + EPS) * g_ref[...]).astype(BF16)

    q_lo = (3 * D_CONV) // IN_TN
    q_hi = (3 * D_CONV + D_ATT) // IN_TN
    col_scale = jnp.where((j >= q_lo) & (j < q_hi), HEAD_DIM ** -0.5, 1.0).astype(F32)
    acc = jnp.dot(hn_ref[...], w_ref[...], preferred_element_type=F32)
    o_ref[...] = (acc * col_scale).astype(o_ref.dtype)


def _in_proj(x, g, w_bf):
    t = x.shape[0]
    return pl.pallas_call(
        _in_proj_kernel,
        grid=(t // IN_TM, D_IN // IN_TN),
        in_specs=[
            pl.BlockSpec((IN_TM, D_MODEL), lambda i, j: (i, 0)),
            pl.BlockSpec((1, D_MODEL), lambda i, j: (0, 0)),
            pl.BlockSpec((D_MODEL, IN_TN), lambda i, j: (0, j)),
        ],
        out_specs=pl.BlockSpec((IN_TM, IN_TN), lambda i, j: (i, j)),
        out_shape=jax.ShapeDtypeStruct((t, D_IN), BF16),
        scratch_shapes=[pltpu.VMEM((IN_TM, D_MODEL), BF16)],
        compiler_params=_cparams(("arbitrary", "arbitrary")),
        name="in_proj",
    )(x, g.reshape(1, D_MODEL), w_bf)


def _mix_kernel(bg_ref, cg_ref, xv_ref, a_ref, gt_ref, wa_ref, wc_ref, bc_ref, lg_ref, lb_ref,
                ya_ref, yc_ref, bufa, bufc, shc):
    s = pl.program_id(1)
    ts = MIX_TS

    @pl.when(s == 0)
    def _():
        bufa[0:HALO_A, :] = jnp.zeros((HALO_A, D_CONV), F32)
        bufc[0:HALO_C, :] = jnp.zeros((HALO_C, D_CFM), F32)

    @pl.when(s > 0)
    def _():
        bufa[0:HALO_A, :] = bufa[ts:ts + HALO_A, :]
        bufc[0:HALO_C, :] = bufc[ts:ts + HALO_C, :]

    bufa[HALO_A:HALO_A + ts, :] = cg_ref[...].astype(F32) * xv_ref[...].astype(F32)
    acc = jnp.zeros((ts, D_CONV), F32)
    for j in range(SHORT_CONV_W):
        off = HALO_A - (SHORT_CONV_W - 1) + j
        acc = acc + wa_ref[j:j + 1, :] * bufa[off:off + ts, :]
    ya_ref[...] = (bg_ref[...].astype(F32) * acc).astype(ya_ref.dtype)

    a = a_ref[...].astype(F32)
    gate = gt_ref[...].astype(F32)
    bufc[HALO_C:HALO_C + ts, :] = a * jax.nn.sigmoid(gate)
    span = ts + HALO_C - SUBLANES
    for r in range(1, SUBLANES):
        shc[r - 1, :, :] = bufc[r:r + span, :]
    acc = jnp.zeros((ts, D_CFM), F32) + bc_ref[...]
    for j in range(CFM_CONV_W):
        off = HALO_C - (CFM_CONV_W - 1) + j
        a, r = divmod(off, SUBLANES)
        win = bufc[off:off + ts, :] if r == 0 else shc[r - 1, a * SUBLANES:a * SUBLANES + ts, :]
        acc = acc + wc_ref[j:j + 1, :] * win
    mu = jnp.mean(acc, axis=-1, keepdims=True)
    cen = acc - mu
    var = jnp.mean(cen * cen, axis=-1, keepdims=True)
    hn = cen * lax.rsqrt(var + EPS) * lg_ref[...] + lb_ref[...]
    yc_ref[...] = (hn * jax.nn.sigmoid(hn)).astype(yc_ref.dtype)


def _mixers(u, wa, wc, bc, lg, lb, batch, seq):
    t = u.shape[0]
    nst = seq // MIX_TS
    row = lambda b, s: b * nst + s
    ublk = lambda c: pl.BlockSpec((MIX_TS, D_CONV), lambda b, s, c=c: (row(b, s), c))
    small = lambda r: pl.BlockSpec((r, D_CONV), lambda b, s: (0, 0))
    c0 = (3 * D_CONV + 3 * D_ATT) // D_CFM
    return pl.pallas_call(
        _mix_kernel,
        grid=(batch, nst),
        in_specs=[ublk(0), ublk(1), ublk(2), ublk(c0), ublk(c0 + 1),
                  small(SHORT_CONV_W), small(CFM_CONV_W), small(1), small(1), small(1)],
        out_specs=[pl.BlockSpec((MIX_TS, D_CONV), lambda b, s: (row(b, s), 0)),
                   pl.BlockSpec((MIX_TS, D_CFM), lambda b, s: (row(b, s), 0))],
        out_shape=[jax.ShapeDtypeStruct((t, D_CONV), BF16), jax.ShapeDtypeStruct((t, D_CFM), BF16)],
        scratch_shapes=[pltpu.VMEM((HALO_A + MIX_TS, D_CONV), F32),
                        pltpu.VMEM((HALO_C + MIX_TS, D_CFM), F32),
                        pltpu.VMEM((SUBLANES - 1, MIX_TS + HALO_C - SUBLANES, D_CFM), F32)],
        compiler_params=_cparams(("arbitrary", "arbitrary")),
        name="mixers",
    )(u, u, u, u, u, wa, wc, bc.reshape(1, D_CFM), lg.reshape(1, D_CFM), lb.reshape(1, D_CFM))


_NT = (((1,), (1,)), ((), ()))

F_MASK = HEAD_DIM
F_BIAS = HEAD_DIM + BF16_ROWS


def _moba_prepare(slope, q_ref, k_ref, v_ref, kaug, vt, qaug, ident):
    blk = MOBA_BLOCK
    seq = k_ref.shape[0]
    nb = seq // blk
    ri = lax.broadcasted_iota(I32, (blk, blk), 0)
    ci = lax.broadcasted_iota(I32, (blk, blk), 1)
    ident[...] = jnp.where(ri == ci, 1.0, 0.0).astype(BF16)
    eye_hd = ident[0:HEAD_DIM, 0:HEAD_DIM]

    kf = k_ref[...].astype(F32).reshape(nb, blk, HEAD_DIM)
    km = jnp.sum(kf, axis=1) * (1.0 / blk)
    km_hi = km.astype(BF16)
    km_lo = (km - km_hi.astype(F32)).astype(BF16)

    kaug[:, 0:HEAD_DIM] = k_ref[...]
    lane = lax.broadcasted_iota(I32, (blk, LANES), 1)
    cc = lax.broadcasted_iota(I32, (blk, LANES), 0).astype(F32)
    fb = F_BIAS - HEAD_DIM

    common = jnp.where((lane == fb) | (lane == fb + 1), 1.0, jnp.where(lane == fb + 2, slope * cc, 0.0))
    for j in range(nb):
        feat = jnp.where(lane == j, 1.0, jnp.where(lane == fb + 3, slope * float(j * blk), common))
        kaug[j * blk:(j + 1) * blk, HEAD_DIM:MOBA_AUG] = feat.astype(BF16)
        vt[j] = lax.dot_general(eye_hd, v_ref[j * blk:(j + 1) * blk, :], _NT,
                                preferred_element_type=F32).astype(BF16)

    q = q_ref[...]
    qaug[0:HEAD_DIM, :] = lax.dot_general(eye_hd, q, _NT, preferred_element_type=F32).astype(BF16)

    gate = (lax.dot_general(km_hi, q, _NT, preferred_element_type=F32)
            + lax.dot_general(km_lo, q, _NT, preferred_element_type=F32))
    sub = lax.broadcasted_iota(I32, (nb, seq), 0)
    tpos = lax.broadcasted_iota(I32, (nb, seq), 1)
    qblk = lax.shift_right_logical(tpos, blk.bit_length() - 1)
    rank = jnp.zeros((nb, seq), F32)
    for jp in range(nb):
        row = gate[jp:jp + 1, :]
        beats = (row > gate) | ((row == gate) & (sub > jp))
        rank = rank + jnp.where(beats & (qblk > jp), 1.0, 0.0)
    allowed = ((sub < qblk) & (rank < MOBA_TOPK)) | (sub == qblk)
    qaug[F_MASK:F_MASK + BF16_ROWS, :] = jnp.where(allowed, 0.0, NEG).astype(BF16)

    rq = (tpos & (blk - 1)).astype(F32)
    bias = jnp.where(sub == 0, -slope * rq, 0.0)
    bias = jnp.where(sub == 1, -slope * (qblk * blk).astype(F32), bias)
    bias = jnp.where((sub == 2) | (sub == 3), 1.0, bias)
    qaug[F_BIAS:F_BIAS + BF16_ROWS, :] = bias.astype(BF16)
    qaug[F_BIAS + BF16_ROWS:MOBA_AUG, :] = jnp.zeros((MOBA_AUG - F_BIAS - BF16_ROWS, seq), BF16)


def _moba_tile(n, kaug, vt, qaug, ident, o_ref):
    blk = MOBA_BLOCK
    qa = qaug[:, n * blk:(n + 1) * blk]
    kc = lax.broadcasted_iota(I32, (blk, blk), 0)
    qr = lax.broadcasted_iota(I32, (blk, blk), 1)
    ss = [jnp.dot(kaug[j * blk:(j + 1) * blk, :], qa, preferred_element_type=F32) for j in range(n + 1)]
    ss[n] = jnp.where(kc <= qr, ss[n], NEG)
    m = jnp.max(ss[0], axis=0, keepdims=True)
    for j in range(1, n + 1):
        m = jnp.maximum(m, jnp.max(ss[j], axis=0, keepdims=True))
    l = jnp.zeros((1, blk), F32)
    acc = jnp.zeros((HEAD_DIM, blk), F32)
    for j in range(n + 1):
        p = jnp.exp(ss[j] - m)
        l = l + jnp.sum(p, axis=0, keepdims=True)
        acc = acc + jnp.dot(vt[j], p.astype(BF16), preferred_element_type=F32)
    out_t = (acc / l).astype(BF16)
    o_ref[n * blk:(n + 1) * blk, :] = lax.dot_general(ident[...], out_t, _NT,
                                                      preferred_element_type=F32).astype(o_ref.dtype)


def _moba_kernel(slopes_ref, q_ref, k_ref, v_ref, o_ref, kaug, vt, qaug, ident):
    step = pl.program_id(2)
    nq = k_ref.shape[0] // MOBA_BLOCK

    @pl.when(step == 0)
    def _():
        _moba_prepare(slopes_ref[pl.program_id(1)], q_ref, k_ref, v_ref, kaug, vt, qaug, ident)

    for s in range(nq // MOBA_TILES_PER_STEP):
        @pl.when(step == s)
        def _(s=s):
            for n in _moba_step_tiles(s, nq):
                _moba_tile(n, kaug, vt, qaug, ident, o_ref)


def _moba_step_tiles(s, nq):
    pairs = MOBA_TILES_PER_STEP // 2
    out = []
    for k in range(s * pairs, (s + 1) * pairs):
        out += [k, nq - 1 - k]
    return out


def _moba(u, slopes, batch, seq):
    t = u.shape[0]
    nq = seq // MOBA_BLOCK
    assert nq == BF16_ROWS
    qc = (3 * D_CONV) // HEAD_DIM
    kc = qc + N_HEADS
    vc = kc + N_HEADS
    whole = lambda c0: pl.BlockSpec((seq, HEAD_DIM), lambda b, h, i, s: (b, c0 + h))
    return pl.pallas_call(
        _moba_kernel,
        grid_spec=pltpu.PrefetchScalarGridSpec(
            num_scalar_prefetch=1,
            grid=(batch, N_HEADS, nq // MOBA_TILES_PER_STEP),
            in_specs=[whole(qc), whole(kc), whole(vc)],
            out_specs=pl.BlockSpec((seq, HEAD_DIM), lambda b, h, i, s: (b, h)),
            scratch_shapes=[
                pltpu.VMEM((seq, MOBA_AUG), BF16),
                pltpu.VMEM((nq, HEAD_DIM, MOBA_BLOCK), BF16),
                pltpu.VMEM((MOBA_AUG, seq), BF16),
                pltpu.VMEM((MOBA_BLOCK, MOBA_BLOCK), BF16),
            ],
        ),
        out_shape=jax.ShapeDtypeStruct((t, D_ATT), BF16),
        compiler_params=_cparams(("arbitrary", "arbitrary", "arbitrary")),
        name="moba",
    )(slopes, u, u, u)


def _first_max(v, rows):
    top = jnp.max(v, axis=0, keepdims=True)
    idx = jnp.min(jnp.where(v == top, rows, float(v.shape[0])), axis=0, keepdims=True)
    return top, idx


def _router(lg, tri_ref, carry_ref):
    tm = lg.shape[1]
    rows = lax.broadcasted_iota(I32, (SUBLANES, tm), 0).astype(F32)
    grp_lg = jnp.where(rows < N_GROUPS, lg[0:SUBLANES, :], NEG)
    ex = jnp.exp(grp_lg - jnp.max(grp_lg, axis=0, keepdims=True))
    pg = ex / jnp.sum(ex, axis=0, keepdims=True)
    p_grp, grp = _first_max(pg, rows)
    le = jnp.zeros((EXPERTS_PER_GROUP, tm), F32)
    for g in range(N_GROUPS):
        r0 = ROUTER_E0 + g * EXPERTS_PER_GROUP
        le = jnp.where(grp == g, lg[r0:r0 + EXPERTS_PER_GROUP, :], le)
    ex = jnp.exp(le - jnp.max(le, axis=0, keepdims=True))
    q = ex / jnp.sum(ex, axis=0, keepdims=True)
    q1, i1 = _first_max(q, rows)
    q2, i2 = _first_max(jnp.where(rows == i1, -1.0, q), rows)
    den = q1 + q2
    g0 = p_grp * q1 / den
    g1 = p_grp * q2 / den
    e0 = grp * EXPERTS_PER_GROUP + i1
    e1 = grp * EXPERTS_PER_GROUP + i2

    erow = lax.broadcasted_iota(I32, (N_EXPERTS, tm), 0).astype(F32)
    is0 = erow == e0
    is1 = erow == e1
    onehot = jnp.where(is0 | is1, 1.0, 0.0)
    before = jnp.dot(onehot.astype(BF16), tri_ref[...], preferred_element_type=F32) + carry_ref[:, 0:1]
    r0 = jnp.sum(jnp.where(is0, before, 0.0), axis=0, keepdims=True)
    r1 = jnp.sum(jnp.where(is1, before, 0.0), axis=0, keepdims=True)
    carry_ref[...] = carry_ref[...] + jnp.sum(onehot, axis=1, keepdims=True)
    return e0, e1, g0, g1, r0, r1


def _out_proj_kernel(x_ref, ya_ref, yb_ref, yc_ref, w_ref, g_ref, wrt_ref, brt_ref,
                     x1_ref, h2_ref, ri_ref, rf_ref, cnt_ref, tri_ref, carry_ref):
    tm = OUT_TM

    @pl.when(pl.program_id(0) == 0)
    def _():
        a = lax.broadcasted_iota(I32, (tm, tm), 0)
        b = lax.broadcasted_iota(I32, (tm, tm), 1)
        tri_ref[...] = jnp.where(a < b, 1.0, 0.0).astype(BF16)
        carry_ref[...] = jnp.zeros(carry_ref.shape, F32)

    acc = x_ref[...]
    acc = acc + jnp.dot(ya_ref[...], w_ref[0:D_CONV, :], preferred_element_type=F32)
    acc = acc + jnp.dot(yb_ref[...], w_ref[D_CONV:D_CONV + D_ATT, :], preferred_element_type=F32)
    acc = acc + jnp.dot(yc_ref[...], w_ref[D_CONV + D_ATT:D_MIX, :], preferred_element_type=F32)
    x1_ref[...] = acc
    ms = jnp.mean(acc * acc, axis=-1, keepdims=True)
    h2 = acc * lax.rsqrt(ms + EPS) * g_ref[...]
    h2_ref[...] = h2

    lg = lax.dot_general(wrt_ref[...], h2.astype(BF16), _NT, preferred_element_type=F32) + brt_ref[...]
    e0, e1, g0, g1, r0, r1 = _router(lg, tri_ref, carry_ref)
    row = lax.broadcasted_iota(I32, (SUBLANES, tm), 0)
    ints = jnp.where(row == 0, e0, jnp.where(row == 1, e1, jnp.where(row == 2, r0, jnp.where(row == 3, r1, 0.0))))
    ri_ref[...] = ints.astype(I32)
    rf_ref[...] = jnp.where(row == 0, g0, jnp.where(row == 1, g1, 0.0))
    cnt_ref[...] = carry_ref[...]


def _out_proj(x, ya, yb, yc, w_bf, g, wrt_bf, brt):
    t = x.shape[0]
    rows = lambda n: pl.BlockSpec((OUT_TM, n), lambda i: (i, 0))
    whole = lambda a, b: pl.BlockSpec((a, b), lambda i: (0, 0))
    cols = pl.BlockSpec((SUBLANES, OUT_TM), lambda i: (0, i))
    return pl.pallas_call(
        _out_proj_kernel,
        grid=(t // OUT_TM,),
        in_specs=[rows(D_MODEL), rows(D_CONV), rows(D_ATT), rows(D_CFM),
                  whole(D_MIX, D_MODEL), whole(1, D_MODEL), whole(ROUTER_ROWS, D_MODEL), whole(ROUTER_ROWS, 1)],
        out_specs=[rows(D_MODEL), rows(D_MODEL), cols, cols, whole(N_EXPERTS, LANES)],
        out_shape=[jax.ShapeDtypeStruct((t, D_MODEL), F32), jax.ShapeDtypeStruct((t, D_MODEL), F32),
                   jax.ShapeDtypeStruct((SUBLANES, t), I32), jax.ShapeDtypeStruct((SUBLANES, t), F32),
                   jax.ShapeDtypeStruct((N_EXPERTS, LANES), F32)],
        scratch_shapes=[pltpu.VMEM((OUT_TM, OUT_TM), BF16), pltpu.VMEM((N_EXPERTS, LANES), F32)],
        compiler_params=_cparams(("arbitrary",)),
        name="out_proj",
    )(x, ya, yb, yc, w_bf, g.reshape(1, D_MODEL), wrt_bf, brt)


def _dispatch_kernel(pad_row_ref, pad_n_ref, nb_ref, dest_ref, h2_ref, xs_hbm, stage, sem, pad_sem):
    i = pl.program_id(0)
    n = pl.num_programs(0)
    slot = lax.rem(i, 2)

    def row_copy(s_idx, r, dst_row, s):
        return pltpu.make_async_copy(stage.at[s_idx, pl.ds(r, 1)], xs_hbm.at[pl.ds(dst_row, 1)], s)

    def tile_wait(s_idx):
        for _ in range(TOPK_IN_GROUP):
            pltpu.make_async_copy(stage.at[s_idx], xs_hbm.at[pl.ds(0, TOK_TM)], sem.at[s_idx]).wait()

    def for_pad_rows(fn):
        def per_expert(e, carry):
            def one(k, c):
                fn(row_copy(0, 0, pad_row_ref[e] + k, pad_sem.at[0]))
                return c
            lax.fori_loop(0, pad_n_ref[e], one, 0)
            return carry
        lax.fori_loop(0, N_EXPERTS, per_expert, 0)

    def for_tail_blocks(fn):
        def one(b, c):
            start = pl.multiple_of(b * FFN_TM, FFN_TM)
            fn(pltpu.make_async_copy(stage.at[0], xs_hbm.at[pl.ds(start, FFN_TM)], pad_sem.at[0]))
            return c
        lax.fori_loop(nb_ref[0], xs_hbm.shape[0] // FFN_TM, one, 0)

    @pl.when(i == 1)
    def _():
        for_pad_rows(lambda cp: cp.wait())
        for_tail_blocks(lambda cp: cp.wait())

    stage[slot] = h2_ref[...]

    @pl.when(i == 0)
    def _():
        for_pad_rows(lambda cp: cp.start())
        for_tail_blocks(lambda cp: cp.start())

    def issue(r, carry):
        row_copy(slot, r, dest_ref[0, 0, r], sem.at[slot]).start(priority=0)
        row_copy(slot, r, dest_ref[0, 0, TOK_TM + r], sem.at[slot]).start(priority=1)
        return carry

    lax.fori_loop(0, TOK_TM, issue, 0, unroll=8)

    @pl.when(i > 0)
    def _():
        tile_wait(1 - slot)

    @pl.when(i == n - 1)
    def _():
        tile_wait(slot)


def _dispatch(h2, dest3, pad_row, pad_n, n_used, n_rows):
    nt = dest3.shape[0]
    assert nt >= 2 and TOK_TM == FFN_TM
    return pl.pallas_call(
        _dispatch_kernel,
        grid_spec=pltpu.PrefetchScalarGridSpec(
            num_scalar_prefetch=3,
            grid=(nt,),
            in_specs=[
                pl.BlockSpec((1, 1, 2 * TOK_TM), lambda i, a, b, c: (i, 0, 0), memory_space=pltpu.SMEM),
                pl.BlockSpec((TOK_TM, D_MODEL), lambda i, a, b, c: (i, 0)),
            ],
            out_specs=pl.BlockSpec(memory_space=pl.ANY),
            scratch_shapes=[pltpu.VMEM((2, TOK_TM, D_MODEL), F32),
                            pltpu.SemaphoreType.DMA((2,)), pltpu.SemaphoreType.DMA((1,))],
        ),
        out_shape=jax.ShapeDtypeStruct((n_rows, D_MODEL), F32),
        compiler_params=_cparams(("arbitrary",)),
        name="dispatch",
    )(pad_row, pad_n, n_used, dest3, h2)


def _ffn_kernel(be_ref, nb_ref, nxt_ref, xs_ref, wg_hbm, wu_hbm, wd_hbm, y_ref,
                stg_g, stg_u, stg_d, wgb, wub, wdb, sem, *, layer):
    i = pl.program_id(0)
    nb = nb_ref[0]

    def weight_copies(e):
        return (pltpu.make_async_copy(wg_hbm.at[layer, e], stg_g, sem.at[0]),
                pltpu.make_async_copy(wu_hbm.at[layer, e], stg_u, sem.at[1]),
                pltpu.make_async_copy(wd_hbm.at[layer, e], stg_d, sem.at[2]))

    @pl.when(i == 0)
    def _():
        for cp in weight_copies(be_ref[0]):
            cp.start()

    @pl.when(i < nb)
    def _():
        e = be_ref[i]
        prev = be_ref[jnp.maximum(i - 1, 0)]

        @pl.when((i == 0) | (e != prev))
        def _():
            for cp in weight_copies(e):
                cp.wait()
            wgb[...] = stg_g[...].astype(BF16)
            wub[...] = stg_u[...].astype(BF16)
            wdb[...] = stg_d[...].astype(BF16)
            nxt = nxt_ref[e]

            @pl.when(nxt < N_EXPERTS)
            def _():
                for cp in weight_copies(nxt):
                    cp.start()

        xb = xs_ref[...].astype(BF16)
        g = jnp.dot(xb, wgb[...], preferred_element_type=F32)
        up = jnp.dot(xb, wub[...], preferred_element_type=F32)
        hid = (g * jax.nn.sigmoid(g) * up).astype(BF16)
        y_ref[...] = jnp.dot(hid, wdb[...], preferred_element_type=F32)

    @pl.when(i >= nb)
    def _():
        y_ref[...] = jnp.zeros(y_ref.shape, y_ref.dtype)


def _ffn(xs, blk_expert, n_used, next_used, w_gate, w_up, w_down, layer):
    n_rows = xs.shape[0]
    n_blocks = n_rows // FFN_TM
    hbm = pl.BlockSpec(memory_space=pl.ANY)
    return pl.pallas_call(
        functools.partial(_ffn_kernel, layer=layer),
        grid_spec=pltpu.PrefetchScalarGridSpec(
            num_scalar_prefetch=3,
            grid=(n_blocks,),
            in_specs=[
                pl.BlockSpec((FFN_TM, D_MODEL), lambda i, be, nb, nx: (jnp.minimum(i, nb[0] - 1), 0)),
                hbm, hbm, hbm,
            ],
            out_specs=pl.BlockSpec((FFN_TM, D_MODEL), lambda i, be, nb, nx: (i, 0)),
            scratch_shapes=[
                pltpu.VMEM((D_MODEL, D_EXPERT), F32),
                pltpu.VMEM((D_MODEL, D_EXPERT), F32),
                pltpu.VMEM((D_EXPERT, D_MODEL), F32),
                pltpu.VMEM((D_MODEL, D_EXPERT), BF16),
                pltpu.VMEM((D_MODEL, D_EXPERT), BF16),
                pltpu.VMEM((D_EXPERT, D_MODEL), BF16),
                pltpu.SemaphoreType.DMA((3,)),
            ],
        ),
        out_shape=jax.ShapeDtypeStruct((n_rows, D_MODEL), F32),
        compiler_params=_cparams(("arbitrary",)),
        name="ffn",
    )(blk_expert, n_used, next_used, xs, w_gate, w_up, w_down)


def _gather_rows(idx_ref, n, src_hbm, dst, sem):
    def body(k, carry):
        for parity in range(2):
            r = 2 * k + parity
            pltpu.make_async_copy(src_hbm.at[pl.ds(idx_ref[0, 0, r], 1)], dst.at[pl.ds(r, 1)],
                                  sem).start(priority=parity)
        return carry
    lax.fori_loop(0, n // 2, body, 0, unroll=4)


def _wait_rows(n, src_hbm, dst, sem):
    pltpu.make_async_copy(src_hbm.at[pl.ds(0, n)], dst, sem).wait()


def _combine_kernel(pos_ref, posn_ref, x1_ref, gt_ref, y_hbm, g_ref, o_ref, ybuf, sem, *, final_norm):
    i = pl.program_id(0)
    n = pl.num_programs(0)
    slot = lax.rem(i, 2)
    tm = TOK_TM

    @pl.when(i == 0)
    def _():
        _gather_rows(pos_ref, 2 * tm, y_hbm, ybuf.at[0], sem.at[0])

    @pl.when(i + 1 < n)
    def _():
        _gather_rows(posn_ref, 2 * tm, y_hbm, ybuf.at[1 - slot], sem.at[1 - slot])

    _wait_rows(2 * tm, y_hbm, ybuf.at[slot], sem.at[slot])
    out = x1_ref[...] + (ybuf[slot, 0:tm, :] * gt_ref[:, 0:1] + ybuf[slot, tm:2 * tm, :] * gt_ref[:, 1:2])
    if final_norm:
        ms = jnp.mean(out * out, axis=-1, keepdims=True)
        out = out * lax.rsqrt(ms + EPS) * g_ref[...]
    o_ref[...] = out


def _combine(x1, y, dest3, gates_t, g, final_norm):
    t = x1.shape[0]
    nt = t // TOK_TM
    return pl.pallas_call(
        functools.partial(_combine_kernel, final_norm=final_norm),
        grid=(nt,),
        in_specs=[
            pl.BlockSpec((1, 1, 2 * TOK_TM), lambda i: (i, 0, 0), memory_space=pltpu.SMEM),
            pl.BlockSpec((1, 1, 2 * TOK_TM), lambda i: (jnp.minimum(i + 1, nt - 1), 0, 0),
                         memory_space=pltpu.SMEM),
            pl.BlockSpec((TOK_TM, D_MODEL), lambda i: (i, 0)),
            pl.BlockSpec((TOK_TM, TOPK_IN_GROUP), lambda i: (i, 0)),
            pl.BlockSpec(memory_space=pl.ANY),
            pl.BlockSpec((1, D_MODEL), lambda i: (0, 0)),
        ],
        out_specs=pl.BlockSpec((TOK_TM, D_MODEL), lambda i: (i, 0)),
        out_shape=jax.ShapeDtypeStruct((t, D_MODEL), F32),
        scratch_shapes=[pltpu.VMEM((2, 2 * TOK_TM, D_MODEL), F32), pltpu.SemaphoreType.DMA((2,))],
        compiler_params=_cparams(("arbitrary",)),
        name="combine",
    )(dest3, dest3, x1, gates_t, y, g.reshape(1, D_MODEL))


def _row_tables(ri, counts_f, t):
    counts = counts_f[:, 0].astype(I32)
    padded = (counts + FFN_TM - 1) // FFN_TM * FFN_TM
    end_p = jnp.cumsum(padded)
    start_p = end_p - padded
    n_rows = t * TOPK_IN_GROUP + N_EXPERTS * FFN_TM
    blk_start = jnp.arange(n_rows // FFN_TM, dtype=I32) * FFN_TM
    blk_expert = jnp.minimum(jnp.sum((end_p[None, :] <= blk_start[:, None]).astype(I32), axis=1), N_EXPERTS - 1)
    n_used = (end_p[-1] // FFN_TM).astype(I32).reshape(1)
    ids = jnp.arange(N_EXPERTS, dtype=I32)
    later = (ids[None, :] > ids[:, None]) & (counts[None, :] > 0)
    next_used = jnp.min(jnp.where(later, ids[None, :], N_EXPERTS), axis=1).astype(I32)
    expert = ri[0:TOPK_IN_GROUP]
    rank = ri[TOPK_IN_GROUP:2 * TOPK_IN_GROUP]
    first = jnp.sum(jnp.where(expert[:, :, None] == jnp.arange(N_EXPERTS, dtype=I32), start_p, 0), axis=-1)
    dest = rank + first
    nt = t // TOK_TM
    dest3 = dest.reshape(TOPK_IN_GROUP, nt, TOK_TM).transpose(1, 0, 2).reshape(nt, 1, TOPK_IN_GROUP * TOK_TM)
    return (blk_expert, n_used, next_used, dest3, (start_p + counts).astype(I32), (padded - counts).astype(I32),
            n_rows)


def _router_weights(rg_w, rg_b, re_w, re_b):
    wrt = jnp.zeros((ROUTER_ROWS, D_MODEL), F32)
    wrt = wrt.at[0:N_GROUPS].set(rg_w.T).at[ROUTER_E0:ROUTER_E0 + N_EXPERTS].set(re_w.T)
    brt = jnp.zeros((ROUTER_ROWS,), F32)
    brt = brt.at[0:N_GROUPS].set(rg_b).at[ROUTER_E0:ROUTER_E0 + N_EXPERTS].set(re_b)
    return wrt.astype(BF16), brt.reshape(ROUTER_ROWS, 1)


def kernel(x, norm1_g, w_in, conv_a_w, conv_c_w, conv_c_b, ln_c_g, ln_c_b, w_out, norm2_g,
           router_g_w, router_g_b, router_e_w, router_e_b, w_gate, w_up, w_down, final_g):
    batch, seq, d = x.shape
    t = batch * seq
    depth = w_in.shape[0]
    xt = x.reshape(t, d)
    slopes = jnp.exp2(-8.0 * jnp.arange(1, N_HEADS + 1, dtype=F32) / N_HEADS)
    for l in range(depth):
        u = _in_proj(xt, norm1_g[l], w_in[l].astype(BF16))
        ya, yc = _mixers(u, conv_a_w[l], conv_c_w[l], conv_c_b[l], ln_c_g[l], ln_c_b[l], batch, seq)
        yb = _moba(u, slopes, batch, seq)
        wrt, brt = _router_weights(router_g_w[l], router_g_b[l], router_e_w[l], router_e_b[l])
        x1, h2, ri, rf, counts = _out_proj(xt, ya, yb, yc, w_out[l].astype(BF16), norm2_g[l], wrt, brt)
        blk_expert, n_used, next_used, dest3, pad_row, pad_n, n_rows = _row_tables(ri, counts, t)
        xs = _dispatch(h2, dest3, pad_row, pad_n, n_used, n_rows)
        y = _ffn(xs, blk_expert, n_used, next_used, w_gate, w_up, w_down, l)
        xt = _combine(x1, y, dest3, rf[0:TOPK_IN_GROUP].T, final_g, final_norm=(l == depth - 1))
    return xt.reshape(batch, seq, d)
```

```python
import functools

import jax
import jax.numpy as jnp
from jax import lax
from jax.experimental import pallas as pl
from jax.experimental.pallas import tpu as pltpu

D_MODEL = 2048
D_CONV = 512
D_ATT = 1024
N_HEADS = 8
HEAD_DIM = 128
D_CFM = 512
D_MIX = D_CONV + D_ATT + D_CFM
D_IN = 3 * D_CONV + 3 * D_ATT + 2 * D_CFM
SHORT_CONV_W = 3
CFM_CONV_W = 31
MOBA_BLOCK = 256
MOBA_TOPK = 3
N_GROUPS = 4
EXPERTS_PER_GROUP = 8
N_EXPERTS = N_GROUPS * EXPERTS_PER_GROUP
TOPK_IN_GROUP = 2
D_EXPERT = 512
EPS = 1e-6
NEG = -1e30

LANES = 128
SUBLANES = 8
BF16_ROWS = 16
VMEM_LIMIT = 56 * 1024 * 1024

IN_TM = 1024
IN_TN = 512
MIX_TS = 512
HALO_A = 8
HALO_C = 32
MOBA_AUG = 2 * HEAD_DIM
MOBA_TILES_PER_STEP = 8
OUT_TM = 512
ROUTER_ROWS = LANES
ROUTER_E0 = SUBLANES
FFN_TM = 256
TOK_TM = 256

BF16 = jnp.bfloat16
F32 = jnp.float32
I32 = jnp.int32


def _cparams(sem):
    return pltpu.CompilerParams(dimension_semantics=sem, vmem_limit_bytes=VMEM_LIMIT)


def _in_proj_kernel(x_ref, g_ref, w_ref, o_ref, hn_ref):
    j = pl.program_id(1)

    @pl.when(j == 0)
    def _():
        x = x_ref[...]
        ms = jnp.mean(x * x, axis=-1, keepdims=True)
        hn_ref[...] = (x * lax.rsqrt(ms + EPS) * g_ref[...]).astype(BF16)

    q_lo = (3 * D_CONV) // IN_TN
    q_hi = (3 * D_CONV + D_ATT) // IN_TN
    col_scale = jnp.where((j >= q_lo) & (j < q_hi), HEAD_DIM ** -0.5, 1.0).astype(F32)
    acc = jnp.dot(hn_ref[...], w_ref[...], preferred_element_type=F32)
    o_ref[...] = (acc * col_scale).astype(o_ref.dtype)


def _in_proj(x, g, w_bf):
    t = x.shape[0]
    return pl.pallas_call(
        _in_proj_kernel,
        grid=(t // IN_TM, D_IN // IN_TN),
        in_specs=[
            pl.BlockSpec((IN_TM, D_MODEL), lambda i, j: (i, 0)),
            pl.BlockSpec((1, D_MODEL), lambda i, j: (0, 0)),
            pl.BlockSpec((D_MODEL, IN_TN), lambda i, j: (0, j)),
        ],
        out_specs=pl.BlockSpec((IN_TM, IN_TN), lambda i, j: (i, j)),
        out_shape=jax.ShapeDtypeStruct((t, D_IN), BF16),
        scratch_shapes=[pltpu.VMEM((IN_TM, D_MODEL), BF16)],
        compiler_params=_cparams(("arbitrary", "arbitrary")),
        name="in_proj",
    )(x, g.reshape(1, D_MODEL), w_bf)


def _mix_kernel(bg_ref, cg_ref, xv_ref, a_ref, gt_ref, wa_ref, wc_ref, bc_ref, lg_ref, lb_ref,
                ya_ref, yc_ref, bufa, bufc, shc):
    s = pl.program_id(1)
    ts = MIX_TS

    @pl.when(s == 0)
    def _():
        bufa[0:HALO_A, :] = jnp.zeros((HALO_A, D_CONV), F32)
        bufc[0:HALO_C, :] = jnp.zeros((HALO_C, D_CFM), F32)

    @pl.when(s > 0)
    def _():
        bufa[0:HALO_A, :] = bufa[ts:ts + HALO_A, :]
        bufc[0:HALO_C, :] = bufc[ts:ts + HALO_C, :]

    bufa[HALO_A:HALO_A + ts, :] = cg_ref[...].astype(F32) * xv_ref[...].astype(F32)
    acc = jnp.zeros((ts, D_CONV), F32)
    for j in range(SHORT_CONV_W):
        off = HALO_A - (SHORT_CONV_W - 1) + j
        acc = acc + wa_ref[j:j + 1, :] * bufa[off:off + ts, :]
    ya_ref[...] = (bg_ref[...].astype(F32) * acc).astype(ya_ref.dtype)

    a = a_ref[...].astype(F32)
    gate = gt_ref[...].astype(F32)
    bufc[HALO_C:HALO_C + ts, :] = a * jax.nn.sigmoid(gate)
    span = ts + HALO_C - SUBLANES
    for r in range(1, SUBLANES):
        shc[r - 1, :, :] = bufc[r:r + span, :]
    acc = jnp.zeros((ts, D_CFM), F32) + bc_ref[...]
    for j in range(CFM_CONV_W):
        off = HALO_C - (CFM_CONV_W - 1) + j
        a, r = divmod(off, SUBLANES)
        win = bufc[off:off + ts, :] if r == 0 else shc[r - 1, a * SUBLANES:a * SUBLANES + ts, :]
        acc = acc + wc_ref[j:j + 1, :] * win
    mu = jnp.mean(acc, axis=-1, keepdims=True)
    cen = acc - mu
    var = jnp.mean(cen * cen, axis=-1, keepdims=True)
    hn = cen * lax.rsqrt(var + EPS) * lg_ref[...] + lb_ref[...]
    yc_ref[...] = (hn * jax.nn.sigmoid(hn)).astype(yc_ref.dtype)


def _mixers(u, wa, wc, bc, lg, lb, batch, seq):
    t = u.shape[0]
    nst = seq // MIX_TS
    row = lambda b, s: b * nst + s
    ublk = lambda c: pl.BlockSpec((MIX_TS, D_CONV), lambda b, s, c=c: (row(b, s), c))
    small = lambda r: pl.BlockSpec((r, D_CONV), lambda b, s: (0, 0))
    c0 = (3 * D_CONV + 3 * D_ATT) // D_CFM
    return pl.pallas_call(
        _mix_kernel,
        grid=(batch, nst),
        in_specs=[ublk(0), ublk(1), ublk(2), ublk(c0), ublk(c0 + 1),
                  small(SHORT_CONV_W), small(CFM_CONV_W), small(1), small(1), small(1)],
        out_specs=[pl.BlockSpec((MIX_TS, D_CONV), lambda b, s: (row(b, s), 0)),
                   pl.BlockSpec((MIX_TS, D_CFM), lambda b, s: (row(b, s), 0))],
        out_shape=[jax.ShapeDtypeStruct((t, D_CONV), BF16), jax.ShapeDtypeStruct((t, D_CFM), BF16)],
        scratch_shapes=[pltpu.VMEM((HALO_A + MIX_TS, D_CONV), F32),
                        pltpu.VMEM((HALO_C + MIX_TS, D_CFM), F32),
                        pltpu.VMEM((SUBLANES - 1, MIX_TS + HALO_C - SUBLANES, D_CFM), F32)],
        compiler_params=_cparams(("arbitrary", "arbitrary")),
        name="mixers",
    )(u, u, u, u, u, wa, wc, bc.reshape(1, D_CFM), lg.reshape(1, D_CFM), lb.reshape(1, D_CFM))


_NT = (((1,), (1,)), ((), ()))

F_MASK = HEAD_DIM
F_BIAS = HEAD_DIM + BF16_ROWS


def _moba_prepare(slope, q_ref, k_ref, v_ref, kaug, vt, qaug, ident):
    blk = MOBA_BLOCK
    seq = k_ref.shape[0]
    nb = seq // blk
    ri = lax.broadcasted_iota(I32, (blk, blk), 0)
    ci = lax.broadcasted_iota(I32, (blk, blk), 1)
    ident[...] = jnp.where(ri == ci, 1.0, 0.0).astype(BF16)
    eye_hd = ident[0:HEAD_DIM, 0:HEAD_DIM]

    kf = k_ref[...].astype(F32).reshape(nb, blk, HEAD_DIM)
    km = jnp.sum(kf, axis=1) * (1.0 / blk)
    km_hi = km.astype(BF16)
    km_lo = (km - km_hi.astype(F32)).astype(BF16)

    kaug[:, 0:HEAD_DIM] = k_ref[...]
    lane = lax.broadcasted_iota(I32, (blk, LANES), 1)
    cc = lax.broadcasted_iota(I32, (blk, LANES), 0).astype(F32)
    fb = F_BIAS - HEAD_DIM

    common = jnp.where((lane == fb) | (lane == fb + 1), 1.0, jnp.where(lane == fb + 2, slope * cc, 0.0))
    for j in range(nb):
        feat = jnp.where(lane == j, 1.0, jnp.where(lane == fb + 3, slope * float(j * blk), common))
        kaug[j * blk:(j + 1) * blk, HEAD_DIM:MOBA_AUG] = feat.astype(BF16)
        vt[j] = lax.dot_general(eye_hd, v_ref[j * blk:(j + 1) * blk, :], _NT,
                                preferred_element_type=F32).astype(BF16)

    q = q_ref[...]
    qaug[0:HEAD_DIM, :] = lax.dot_general(eye_hd, q, _NT, preferred_element_type=F32).astype(BF16)

    gate = (lax.dot_general(km_hi, q, _NT, preferred_element_type=F32)
            + lax.dot_general(km_lo, q, _NT, preferred_element_type=F32))
    sub = lax.broadcasted_iota(I32, (nb, seq), 0)
    tpos = lax.broadcasted_iota(I32, (nb, seq), 1)
    qblk = lax.shift_right_logical(tpos, blk.bit_length() - 1)
    rank = jnp.zeros((nb, seq), F32)
    for jp in range(nb):
        row = gate[jp:jp + 1, :]
        beats = (row > gate) | ((row == gate) & (sub > jp))
        rank = rank + jnp.where(beats & (qblk > jp), 1.0, 0.0)
    allowed = ((sub < qblk) & (rank < MOBA_TOPK)) | (sub == qblk)
    qaug[F_MASK:F_MASK + BF16_ROWS, :] = jnp.where(allowed, 0.0, NEG).astype(BF16)

    rq = (tpos & (blk - 1)).astype(F32)
    bias = jnp.where(sub == 0, -slope * rq, 0.0)
    bias = jnp.where(sub == 1, -slope * (qblk * blk).astype(F32), bias)
    bias = jnp.where((sub == 2) | (sub == 3), 1.0, bias)
    qaug[F_BIAS:F_BIAS + BF16_ROWS, :] = bias.astype(BF16)
    qaug[F_BIAS + BF16_ROWS:MOBA_AUG, :] = jnp.zeros((MOBA_AUG - F_BIAS - BF16_ROWS, seq), BF16)


def _moba_tile(n, kaug, vt, qaug, ident, o_ref):
    blk = MOBA_BLOCK
    qa = qaug[:, n * blk:(n + 1) * blk]
    kc = lax.broadcasted_iota(I32, (blk, blk), 0)
    qr = lax.broadcasted_iota(I32, (blk, blk), 1)
    ss = [jnp.dot(kaug[j * blk:(j + 1) * blk, :], qa, preferred_element_type=F32) for j in range(n + 1)]
    ss[n] = jnp.where(kc <= qr, ss[n], NEG)
    m = jnp.max(ss[0], axis=0, keepdims=True)
    for j in range(1, n + 1):
        m = jnp.maximum(m, jnp.max(ss[j], axis=0, keepdims=True))
    l = jnp.zeros((1, blk), F32)
    acc = jnp.zeros((HEAD_DIM, blk), F32)
    for j in range(n + 1):
        p = jnp.exp(ss[j] - m)
        l = l + jnp.sum(p, axis=0, keepdims=True)
        acc = acc + jnp.dot(vt[j], p.astype(BF16), preferred_element_type=F32)
    out_t = (acc / l).astype(BF16)
    o_ref[n * blk:(n + 1) * blk, :] = lax.dot_general(ident[...], out_t, _NT,
                                                      preferred_element_type=F32).astype(o_ref.dtype)


def _moba_kernel(slopes_ref, q_ref, k_ref, v_ref, o_ref, kaug, vt, qaug, ident):
    step = pl.program_id(2)
    nq = k_ref.shape[0] // MOBA_BLOCK

    @pl.when(step == 0)
    def _():
        _moba_prepare(slopes_ref[pl.program_id(1)], q_ref, k_ref, v_ref, kaug, vt, qaug, ident)

    for s in range(nq // MOBA_TILES_PER_STEP):
        @pl.when(step == s)
        def _(s=s):
            for n in _moba_step_tiles(s, nq):
                _moba_tile(n, kaug, vt, qaug, ident, o_ref)


def _moba_step_tiles(s, nq):
    pairs = MOBA_TILES_PER_STEP // 2
    out = []
    for k in range(s * pairs, (s + 1) * pairs):
        out += [k, nq - 1 - k]
    return out


def _moba(u, slopes, batch, seq):
    t = u.shape[0]
    nq = seq // MOBA_BLOCK
    assert nq == BF16_ROWS
    qc = (3 * D_CONV) // HEAD_DIM
    kc = qc + N_HEADS
    vc = kc + N_HEADS
    whole = lambda c0: pl.BlockSpec((seq, HEAD_DIM), lambda b, h, i, s: (b, c0 + h))
    return pl.pallas_call(
        _moba_kernel,
        grid_spec=pltpu.PrefetchScalarGridSpec(
            num_scalar_prefetch=1,
            grid=(batch, N_HEADS, nq // MOBA_TILES_PER_STEP),
            in_specs=[whole(qc), whole(kc), whole(vc)],
            out_specs=pl.BlockSpec((seq, HEAD_DIM), lambda b, h, i, s: (b, h)),
            scratch_shapes=[
                pltpu.VMEM((seq, MOBA_AUG), BF16),
                pltpu.VMEM((nq, HEAD_DIM, MOBA_BLOCK), BF16),
                pltpu.VMEM((MOBA_AUG, seq), BF16),
                pltpu.VMEM((MOBA_BLOCK, MOBA_BLOCK), BF16),
            ],
        ),
        out_shape=jax.ShapeDtypeStruct((t, D_ATT), BF16),
        compiler_params=_cparams(("arbitrary", "arbitrary", "arbitrary")),
        name="moba",
    )(slopes, u, u, u)


def _first_max(v, rows):
    top = jnp.max(v, axis=0, keepdims=True)
    idx = jnp.min(jnp.where(v == top, rows, float(v.shape[0])), axis=0, keepdims=True)
    return top, idx


def _router(lg, tri_ref, carry_ref):
    tm = lg.shape[1]
    rows = lax.broadcasted_iota(I32, (SUBLANES, tm), 0).astype(F32)
    grp_lg = jnp.where(rows < N_GROUPS, lg[0:SUBLANES, :], NEG)
    ex = jnp.exp(grp_lg - jnp.max(grp_lg, axis=0, keepdims=True))
    pg = ex / jnp.sum(ex, axis=0, keepdims=True)
    p_grp, grp = _first_max(pg, rows)
    le = jnp.zeros((EXPERTS_PER_GROUP, tm), F32)
    for g in range(N_GROUPS):
        r0 = ROUTER_E0 + g * EXPERTS_PER_GROUP
        le = jnp.where(grp == g, lg[r0:r0 + EXPERTS_PER_GROUP, :], le)
    ex = jnp.exp(le - jnp.max(le, axis=0, keepdims=True))
    q = ex / jnp.sum(ex, axis=0, keepdims=True)
    q1, i1 = _first_max(q, rows)
    q2, i2 = _first_max(jnp.where(rows == i1, -1.0, q), rows)
    den = q1 + q2
    g0 = p_grp * q1 / den
    g1 = p_grp * q2 / den
    e0 = grp * EXPERTS_PER_GROUP + i1
    e1 = grp * EXPERTS_PER_GROUP + i2

    erow = lax.broadcasted_iota(I32, (N_EXPERTS, tm), 0).astype(F32)
    is0 = erow == e0
    is1 = erow == e1
    onehot = jnp.where(is0 | is1, 1.0, 0.0)
    before = jnp.dot(onehot.astype(BF16), tri_ref[...], preferred_element_type=F32) + carry_ref[:, 0:1]
    r0 = jnp.sum(jnp.where(is0, before, 0.0), axis=0, keepdims=True)
    r1 = jnp.sum(jnp.where(is1, before, 0.0), axis=0, keepdims=True)
    carry_ref[...] = carry_ref[...] + jnp.sum(onehot, axis=1, keepdims=True)
    return e0, e1, g0, g1, r0, r1


def _out_proj_kernel(x_ref, ya_ref, yb_ref, yc_ref, w_ref, g_ref, wrt_ref, brt_ref,
                     x1_ref, h2_ref, ri_ref, rf_ref, cnt_ref, tri_ref, carry_ref):
    tm = OUT_TM

    @pl.when(pl.program_id(0) == 0)
    def _():
        a = lax.broadcasted_iota(I32, (tm, tm), 0)
        b = lax.broadcasted_iota(I32, (tm, tm), 1)
        tri_ref[...] = jnp.where(a < b, 1.0, 0.0).astype(BF16)
        carry_ref[...] = jnp.zeros(carry_ref.shape, F32)

    acc = x_ref[...]
    acc = acc + jnp.dot(ya_ref[...], w_ref[0:D_CONV, :], preferred_element_type=F32)
    acc = acc + jnp.dot(yb_ref[...], w_ref[D_CONV:D_CONV + D_ATT, :], preferred_element_type=F32)
    acc = acc + jnp.dot(yc_ref[...], w_ref[D_CONV + D_ATT:D_MIX, :], preferred_element_type=F32)
    x1_ref[...] = acc
    ms = jnp.mean(acc * acc, axis=-1, keepdims=True)
    h2 = acc * lax.rsqrt(ms + EPS) * g_ref[...]
    h2_ref[...] = h2

    lg = lax.dot_general(wrt_ref[...], h2.astype(BF16), _NT, preferred_element_type=F32) + brt_ref[...]
    e0, e1, g0, g1, r0, r1 = _router(lg, tri_ref, carry_ref)
    row = lax.broadcasted_iota(I32, (SUBLANES, tm), 0)
    ints = jnp.where(row == 0, e0, jnp.where(row == 1, e1, jnp.where(row == 2, r0, jnp.where(row == 3, r1, 0.0))))
    ri_ref[...] = ints.astype(I32)
    rf_ref[...] = jnp.where(row == 0, g0, jnp.where(row == 1, g1, 0.0))
    cnt_ref[...] = carry_ref[...]


def _out_proj(x, ya, yb, yc, w_bf, g, wrt_bf, brt):
    t = x.shape[0]
    rows = lambda n: pl.BlockSpec((OUT_TM, n), lambda i: (i, 0))
    whole = lambda a, b: pl.BlockSpec((a, b), lambda i: (0, 0))
    cols = pl.BlockSpec((SUBLANES, OUT_TM), lambda i: (0, i))
    return pl.pallas_call(
        _out_proj_kernel,
        grid=(t // OUT_TM,),
        in_specs=[rows(D_MODEL), rows(D_CONV), rows(D_ATT), rows(D_CFM),
                  pl.BlockSpec((D_MIX, D_MODEL), lambda i: (0, 0), pipeline_mode=pl.Buffered(1)),
                  whole(1, D_MODEL), whole(ROUTER_ROWS, D_MODEL), whole(ROUTER_ROWS, 1)],
        out_specs=[rows(D_MODEL), rows(D_MODEL), cols, cols, whole(N_EXPERTS, LANES)],
        out_shape=[jax.ShapeDtypeStruct((t, D_MODEL), F32), jax.ShapeDtypeStruct((t, D_MODEL), F32),
                   jax.ShapeDtypeStruct((SUBLANES, t), I32), jax.ShapeDtypeStruct((SUBLANES, t), F32),
                   jax.ShapeDtypeStruct((N_EXPERTS, LANES), F32)],
        scratch_shapes=[pltpu.VMEM((OUT_TM, OUT_TM), BF16), pltpu.VMEM((N_EXPERTS, LANES), F32)],
        compiler_params=_cparams(("arbitrary",)),
        name="out_proj",
    )(x, ya, yb, yc, w_bf, g.reshape(1, D_MODEL), wrt_bf, brt)


def _dispatch_kernel(pad_row_ref, pad_n_ref, nb_ref, dest_ref, h2_ref, xs_hbm, stage, sem, pad_sem):
    i = pl.program_id(0)
    n = pl.num_programs(0)
    slot = lax.rem(i, 2)

    def row_copy(s_idx, r, dst_row, s):
        return pltpu.make_async_copy(stage.at[s_idx, pl.ds(r, 1)], xs_hbm.at[pl.ds(dst_row, 1)], s)

    def tile_wait(s_idx):
        for _ in range(TOPK_IN_GROUP):
            pltpu.make_async_copy(stage.at[s_idx], xs_hbm.at[pl.ds(0, TOK_TM)], sem.at[s_idx]).wait()

    def for_pad_rows(fn):
        def per_expert(e, carry):
            def one(k, c):
                fn(row_copy(0, 0, pad_row_ref[e] + k, pad_sem.at[0]))
                return c
            lax.fori_loop(0, pad_n_ref[e], one, 0)
            return carry
        lax.fori_loop(0, N_EXPERTS, per_expert, 0)

    def for_tail_blocks(fn):
        def one(b, c):
            start = pl.multiple_of(b * FFN_TM, FFN_TM)
            fn(pltpu.make_async_copy(stage.at[0], xs_hbm.at[pl.ds(start, FFN_TM)], pad_sem.at[0]))
            return c
        lax.fori_loop(nb_ref[0], xs_hbm.shape[0] // FFN_TM, one, 0)

    @pl.when(i == 1)
    def _():
        for_pad_rows(lambda cp: cp.wait())
        for_tail_blocks(lambda cp: cp.wait())

    stage[slot] = h2_ref[...]

    @pl.when(i == 0)
    def _():
        for_pad_rows(lambda cp: cp.start())
        for_tail_blocks(lambda cp: cp.start())

    def issue(r, carry):
        row_copy(slot, r, dest_ref[0, 0, r], sem.at[slot]).start(priority=0)
        row_copy(slot, r, dest_ref[0, 0, TOK_TM + r], sem.at[slot]).start(priority=1)
        return carry

    lax.fori_loop(0, TOK_TM, issue, 0, unroll=8)

    @pl.when(i > 0)
    def _():
        tile_wait(1 - slot)

    @pl.when(i == n - 1)
    def _():
        tile_wait(slot)


def _dispatch(h2, dest3, pad_row, pad_n, n_used, n_rows):
    nt = dest3.shape[0]
    assert nt >= 2 and TOK_TM == FFN_TM
    return pl.pallas_call(
        _dispatch_kernel,
        grid_spec=pltpu.PrefetchScalarGridSpec(
            num_scalar_prefetch=3,
            grid=(nt,),
            in_specs=[
                pl.BlockSpec((1, 1, 2 * TOK_TM), lambda i, a, b, c: (i, 0, 0), memory_space=pltpu.SMEM),
                pl.BlockSpec((TOK_TM, D_MODEL), lambda i, a, b, c: (i, 0)),
            ],
            out_specs=pl.BlockSpec(memory_space=pl.ANY),
            scratch_shapes=[pltpu.VMEM((2, TOK_TM, D_MODEL), F32),
                            pltpu.SemaphoreType.DMA((2,)), pltpu.SemaphoreType.DMA((1,))],
        ),
        out_shape=jax.ShapeDtypeStruct((n_rows, D_MODEL), F32),
        compiler_params=_cparams(("arbitrary",)),
        name="dispatch",
    )(pad_row, pad_n, n_used, dest3, h2)


def _ffn_kernel(be_ref, nb_ref, nxt_ref, xs_ref, wg_hbm, wu_hbm, wd_hbm, y_ref,
                stg_g, stg_u, stg_d, wgb, wub, wdb, sem, *, layer):
    i = pl.program_id(0)
    nb = nb_ref[0]

    def weight_copies(e):
        return (pltpu.make_async_copy(wg_hbm.at[layer, e], stg_g, sem.at[0]),
                pltpu.make_async_copy(wu_hbm.at[layer, e], stg_u, sem.at[1]),
                pltpu.make_async_copy(wd_hbm.at[layer, e], stg_d, sem.at[2]))

    @pl.when(i == 0)
    def _():
        for cp in weight_copies(be_ref[0]):
            cp.start()

    @pl.when(i < nb)
    def _():
        e = be_ref[i]
        prev = be_ref[jnp.maximum(i - 1, 0)]

        @pl.when((i == 0) | (e != prev))
        def _():
            for cp in weight_copies(e):
                cp.wait()
            wgb[...] = stg_g[...].astype(BF16)
            wub[...] = stg_u[...].astype(BF16)
            wdb[...] = stg_d[...].astype(BF16)
            nxt = nxt_ref[e]

            @pl.when(nxt < N_EXPERTS)
            def _():
                for cp in weight_copies(nxt):
                    cp.start()

        xb = xs_ref[...].astype(BF16)
        g = jnp.dot(xb, wgb[...], preferred_element_type=F32)
        up = jnp.dot(xb, wub[...], preferred_element_type=F32)
        hid = (g * jax.nn.sigmoid(g) * up).astype(BF16)
        y_ref[...] = jnp.dot(hid, wdb[...], preferred_element_type=F32)

    @pl.when(i >= nb)
    def _():
        y_ref[...] = jnp.zeros(y_ref.shape, y_ref.dtype)


def _ffn(xs, blk_expert, n_used, next_used, w_gate, w_up, w_down, layer):
    n_rows = xs.shape[0]
    n_blocks = n_rows // FFN_TM
    hbm = pl.BlockSpec(memory_space=pl.ANY)
    return pl.pallas_call(
        functools.partial(_ffn_kernel, layer=layer),
        grid_spec=pltpu.PrefetchScalarGridSpec(
            num_scalar_prefetch=3,
            grid=(n_blocks,),
            in_specs=[
                pl.BlockSpec((FFN_TM, D_MODEL), lambda i, be, nb, nx: (jnp.minimum(i, nb[0] - 1), 0)),
                hbm, hbm, hbm,
            ],
            out_specs=pl.BlockSpec((FFN_TM, D_MODEL), lambda i, be, nb, nx: (i, 0)),
            scratch_shapes=[
                pltpu.VMEM((D_MODEL, D_EXPERT), F32),
                pltpu.VMEM((D_MODEL, D_EXPERT), F32),
                pltpu.VMEM((D_EXPERT, D_MODEL), F32),
                pltpu.VMEM((D_MODEL, D_EXPERT), BF16),
                pltpu.VMEM((D_MODEL, D_EXPERT), BF16),
                pltpu.VMEM((D_EXPERT, D_MODEL), BF16),
                pltpu.SemaphoreType.DMA((3,)),
            ],
        ),
        out_shape=jax.ShapeDtypeStruct((n_rows, D_MODEL), F32),
        compiler_params=_cparams(("arbitrary",)),
        name="ffn",
    )(blk_expert, n_used, next_used, xs, w_gate, w_up, w_down)


def _gather_rows(idx_ref, n, src_hbm, dst, sem):
    def body(k, carry):
        for parity in range(2):
            r = 2 * k + parity
            pltpu.make_async_copy(src_hbm.at[pl.ds(idx_ref[0, 0, r], 1)], dst.at[pl.ds(r, 1)],
                                  sem).start(priority=parity)
        return carry
    lax.fori_loop(0, n // 2, body, 0, unroll=4)


def _wait_rows(n, src_hbm, dst, sem):
    pltpu.make_async_copy(src_hbm.at[pl.ds(0, n)], dst, sem).wait()


def _combine_kernel(pos_ref, posn_ref, x1_ref, gt_ref, y_hbm, g_ref, o_ref, ybuf, sem, *, final_norm):
    i = pl.program_id(0)
    n = pl.num_programs(0)
    slot = lax.rem(i, 2)
    tm = TOK_TM

    @pl.when(i == 0)
    def _():
        _gather_rows(pos_ref, 2 * tm, y_hbm, ybuf.at[0], sem.at[0])

    @pl.when(i + 1 < n)
    def _():
        _gather_rows(posn_ref, 2 * tm, y_hbm, ybuf.at[1 - slot], sem.at[1 - slot])

    _wait_rows(2 * tm, y_hbm, ybuf.at[slot], sem.at[slot])
    out = x1_ref[...] + (ybuf[slot, 0:tm, :] * gt_ref[:, 0:1] + ybuf[slot, tm:2 * tm, :] * gt_ref[:, 1:2])
    if final_norm:
        ms = jnp.mean(out * out, axis=-1, keepdims=True)
        out = out * lax.rsqrt(ms + EPS) * g_ref[...]
    o_ref[...] = out


def _combine(x1, y, dest3, gates_t, g, final_norm):
    t = x1.shape[0]
    nt = t // TOK_TM
    return pl.pallas_call(
        functools.partial(_combine_kernel, final_norm=final_norm),
        grid=(nt,),
        in_specs=[
            pl.BlockSpec((1, 1, 2 * TOK_TM), lambda i: (i, 0, 0), memory_space=pltpu.SMEM),
            pl.BlockSpec((1, 1, 2 * TOK_TM), lambda i: (jnp.minimum(i + 1, nt - 1), 0, 0),
                         memory_space=pltpu.SMEM),
            pl.BlockSpec((TOK_TM, D_MODEL), lambda i: (i, 0)),
            pl.BlockSpec((TOK_TM, TOPK_IN_GROUP), lambda i: (i, 0)),
            pl.BlockSpec(memory_space=pl.ANY),
            pl.BlockSpec((1, D_MODEL), lambda i: (0, 0)),
        ],
        out_specs=pl.BlockSpec((TOK_TM, D_MODEL), lambda i: (i, 0)),
        out_shape=jax.ShapeDtypeStruct((t, D_MODEL), F32),
        scratch_shapes=[pltpu.VMEM((2, 2 * TOK_TM, D_MODEL), F32), pltpu.SemaphoreType.DMA((2,))],
        compiler_params=_cparams(("arbitrary",)),
        name="combine",
    )(dest3, dest3, x1, gates_t, y, g.reshape(1, D_MODEL))


def _row_tables(ri, counts_f, t):
    counts = counts_f[:, 0].astype(I32)
    padded = (counts + FFN_TM - 1) // FFN_TM * FFN_TM
    end_p = jnp.cumsum(padded)
    start_p = end_p - padded
    n_rows = t * TOPK_IN_GROUP + N_EXPERTS * FFN_TM
    blk_start = jnp.arange(n_rows // FFN_TM, dtype=I32) * FFN_TM
    blk_expert = jnp.minimum(jnp.sum((end_p[None, :] <= blk_start[:, None]).astype(I32), axis=1), N_EXPERTS - 1)
    n_used = (end_p[-1] // FFN_TM).astype(I32).reshape(1)
    ids = jnp.arange(N_EXPERTS, dtype=I32)
    later = (ids[None, :] > ids[:, None]) & (counts[None, :] > 0)
    next_used = jnp.min(jnp.where(later, ids[None, :], N_EXPERTS), axis=1).astype(I32)
    expert = ri[0:TOPK_IN_GROUP]
    rank = ri[TOPK_IN_GROUP:2 * TOPK_IN_GROUP]
    first = jnp.sum(jnp.where(expert[:, :, None] == jnp.arange(N_EXPERTS, dtype=I32), start_p, 0), axis=-1)
    dest = rank + first
    nt = t // TOK_TM
    dest3 = dest.reshape(TOPK_IN_GROUP, nt, TOK_TM).transpose(1, 0, 2).reshape(nt, 1, TOPK_IN_GROUP * TOK_TM)
    return (blk_expert, n_used, next_used, dest3, (start_p + counts).astype(I32), (padded - counts).astype(I32),
            n_rows)


def _router_weights(rg_w, rg_b, re_w, re_b):
    wrt = jnp.zeros((ROUTER_ROWS, D_MODEL), F32)
    wrt = wrt.at[0:N_GROUPS].set(rg_w.T).at[ROUTER_E0:ROUTER_E0 + N_EXPERTS].set(re_w.T)
    brt = jnp.zeros((ROUTER_ROWS,), F32)
    brt = brt.at[0:N_GROUPS].set(rg_b).at[ROUTER_E0:ROUTER_E0 + N_EXPERTS].set(re_b)
    return wrt.astype(BF16), brt.reshape(ROUTER_ROWS, 1)


def kernel(x, norm1_g, w_in, conv_a_w, conv_c_w, conv_c_b, ln_c_g, ln_c_b, w_out, norm2_g,
           router_g_w, router_g_b, router_e_w, router_e_b, w_gate, w_up, w_down, final_g):
    batch, seq, d = x.shape
    t = batch * seq
    depth = w_in.shape[0]
    xt = x.reshape(t, d)
    slopes = jnp.exp2(-8.0 * jnp.arange(1, N_HEADS + 1, dtype=F32) / N_HEADS)
    for l in range(depth):
        u = _in_proj(xt, norm1_g[l], w_in[l].astype(BF16))
        ya, yc = _mixers(u, conv_a_w[l], conv_c_w[l], conv_c_b[l], ln_c_g[l], ln_c_b[l], batch, seq)
        yb = _moba(u, slopes, batch, seq)
        wrt, brt = _router_weights(router_g_w[l], router_g_b[l], router_e_w[l], router_e_b[l])
        x1, h2, ri, rf, counts = _out_proj(xt, ya, yb, yc, w_out[l].astype(BF16), norm2_g[l], wrt, brt)
        blk_expert, n_used, next_used, dest3, pad_row, pad_n, n_rows = _row_tables(ri, counts, t)
        xs = _dispatch(h2, dest3, pad_row, pad_n, n_used, n_rows)
        y = _ffn(xs, blk_expert, n_used, next_used, w_gate, w_up, w_down, l)
        xt = _combine(x1, y, dest3, rf[0:TOPK_IN_GROUP].T, final_g, final_norm=(l == depth - 1))
    return xt.reshape(batch, seq, d)
```

```python
import functools

import jax
import jax.numpy as jnp
from jax import lax
from jax.experimental import pallas as pl
from jax.experimental.pallas import tpu as pltpu

D_MODEL = 2048
D_CONV = 512
D_ATT = 1024
N_HEADS = 8
HEAD_DIM = 128
D_CFM = 512
D_MIX = D_CONV + D_ATT + D_CFM
D_IN = 3 * D_CONV + 3 * D_ATT + 2 * D_CFM
SHORT_CONV_W = 3
CFM_CONV_W = 31
MOBA_BLOCK = 256
MOBA_TOPK = 3
N_GROUPS = 4
EXPERTS_PER_GROUP = 8
N_EXPERTS = N_GROUPS * EXPERTS_PER_GROUP
TOPK_IN_GROUP = 2
D_EXPERT = 512
EPS = 1e-6
NEG = -1e30

LANES = 128
SUBLANES = 8
BF16_ROWS = 16
VMEM_LIMIT = 56 * 1024 * 1024

IN_TM = 512
IN_TN = D_IN // 2
MIX_TS = 512
HALO_A = 8
HALO_C = 32
MOBA_AUG = 2 * HEAD_DIM
MOBA_TILES_PER_STEP = 16
OUT_TM = 512
ROUTER_ROWS = LANES
ROUTER_E0 = SUBLANES
FFN_TM = 256
TOK_TM = 512

BF16 = jnp.bfloat16
F32 = jnp.float32
I32 = jnp.int32


def _cparams(sem):
    return pltpu.CompilerParams(dimension_semantics=sem, vmem_limit_bytes=VMEM_LIMIT)


def _in_proj_kernel(x_ref, g_ref, w_ref, o_ref, hn_ref):
    j = pl.program_id(1)

    @pl.when(j == 0)
    def _():
        x = x_ref[...]
        ms = jnp.mean(x * x, axis=-1, keepdims=True)
        hn_ref[...] = (x * lax.rsqrt(ms + EPS) * g_ref[...]).astype(BF16)

    col = j * IN_TN + lax.broadcasted_iota(I32, (1, IN_TN), 1)
    is_q = (col >= 3 * D_CONV) & (col < 3 * D_CONV + D_ATT)
    col_scale = jnp.where(is_q, HEAD_DIM ** -0.5, 1.0).astype(F32)
    acc = jnp.dot(hn_ref[...], w_ref[...], preferred_element_type=F32)
    o_ref[...] = (acc * col_scale).astype(o_ref.dtype)


def _in_proj(x, g, w_bf):
    t = x.shape[0]
    return pl.pallas_call(
        _in_proj_kernel,
        grid=(t // IN_TM, D_IN // IN_TN),
        in_specs=[
            pl.BlockSpec((IN_TM, D_MODEL), lambda i, j: (i, 0)),
            pl.BlockSpec((1, D_MODEL), lambda i, j: (0, 0)),
            pl.BlockSpec((D_MODEL, IN_TN), lambda i, j: (0, j)),
        ],
        out_specs=pl.BlockSpec((IN_TM, IN_TN), lambda i, j: (i, j)),
        out_shape=jax.ShapeDtypeStruct((t, D_IN), BF16),
        scratch_shapes=[pltpu.VMEM((IN_TM, D_MODEL), BF16)],
        compiler_params=_cparams(("arbitrary", "arbitrary")),
        name="in_proj",
    )(x, g.reshape(1, D_MODEL), w_bf)


def _mix_kernel(bg_ref, cg_ref, xv_ref, a_ref, gt_ref, wa_ref, wc_ref, bc_ref, lg_ref, lb_ref,
                ya_ref, yc_ref, bufa, bufc, shc):
    s = pl.program_id(1)
    ts = MIX_TS

    @pl.when(s == 0)
    def _():
        bufa[0:HALO_A, :] = jnp.zeros((HALO_A, D_CONV), F32)
        bufc[0:HALO_C, :] = jnp.zeros((HALO_C, D_CFM), F32)

    @pl.when(s > 0)
    def _():
        bufa[0:HALO_A, :] = bufa[ts:ts + HALO_A, :]
        bufc[0:HALO_C, :] = bufc[ts:ts + HALO_C, :]

    bufa[HALO_A:HALO_A + ts, :] = cg_ref[...].astype(F32) * xv_ref[...].astype(F32)
    acc = jnp.zeros((ts, D_CONV), F32)
    for j in range(SHORT_CONV_W):
        off = HALO_A - (SHORT_CONV_W - 1) + j
        acc = acc + wa_ref[j:j + 1, :] * bufa[off:off + ts, :]
    ya_ref[...] = (bg_ref[...].astype(F32) * acc).astype(ya_ref.dtype)

    a = a_ref[...].astype(F32)
    gate = gt_ref[...].astype(F32)
    bufc[HALO_C:HALO_C + ts, :] = a * jax.nn.sigmoid(gate)
    span = ts + HALO_C - SUBLANES
    for r in range(1, SUBLANES):
        shc[r - 1, :, :] = bufc[r:r + span, :]
    acc = jnp.zeros((ts, D_CFM), F32) + bc_ref[...]
    for j in range(CFM_CONV_W):
        off = HALO_C - (CFM_CONV_W - 1) + j
        a, r = divmod(off, SUBLANES)
        win = bufc[off:off + ts, :] if r == 0 else shc[r - 1, a * SUBLANES:a * SUBLANES + ts, :]
        acc = acc + wc_ref[j:j + 1, :] * win
    mu = jnp.mean(acc, axis=-1, keepdims=True)
    cen = acc - mu
    var = jnp.mean(cen * cen, axis=-1, keepdims=True)
    hn = cen * lax.rsqrt(var + EPS) * lg_ref[...] + lb_ref[...]
    yc_ref[...] = (hn * jax.nn.sigmoid(hn)).astype(yc_ref.dtype)


def _mixers(u, wa, wc, bc, lg, lb, batch, seq):
    t = u.shape[0]
    nst = seq // MIX_TS
    row = lambda b, s: b * nst + s
    ublk = lambda c: pl.BlockSpec((MIX_TS, D_CONV), lambda b, s, c=c: (row(b, s), c))
    small = lambda r: pl.BlockSpec((r, D_CONV), lambda b, s: (0, 0))
    c0 = (3 * D_CONV + 3 * D_ATT) // D_CFM
    return pl.pallas_call(
        _mix_kernel,
        grid=(batch, nst),
        in_specs=[ublk(0), ublk(1), ublk(2), ublk(c0), ublk(c0 + 1),
                  small(SHORT_CONV_W), small(CFM_CONV_W), small(1), small(1), small(1)],
        out_specs=[pl.BlockSpec((MIX_TS, D_CONV), lambda b, s: (row(b, s), 0)),
                   pl.BlockSpec((MIX_TS, D_CFM), lambda b, s: (row(b, s), 0))],
        out_shape=[jax.ShapeDtypeStruct((t, D_CONV), BF16), jax.ShapeDtypeStruct((t, D_CFM), BF16)],
        scratch_shapes=[pltpu.VMEM((HALO_A + MIX_TS, D_CONV), F32),
                        pltpu.VMEM((HALO_C + MIX_TS, D_CFM), F32),
                        pltpu.VMEM((SUBLANES - 1, MIX_TS + HALO_C - SUBLANES, D_CFM), F32)],
        compiler_params=_cparams(("arbitrary", "arbitrary")),
        name="mixers",
    )(u, u, u, u, u, wa, wc, bc.reshape(1, D_CFM), lg.reshape(1, D_CFM), lb.reshape(1, D_CFM))


_NT = (((1,), (1,)), ((), ()))

F_MASK = HEAD_DIM
F_BIAS = HEAD_DIM + BF16_ROWS


def _moba_prepare(slope, q_ref, k_ref, v_ref, kaug, vt, qaug, ident):
    blk = MOBA_BLOCK
    seq = k_ref.shape[0]
    nb = seq // blk
    ri = lax.broadcasted_iota(I32, (blk, blk), 0)
    ci = lax.broadcasted_iota(I32, (blk, blk), 1)
    ident[...] = jnp.where(ri == ci, 1.0, 0.0).astype(BF16)
    eye_hd = ident[0:HEAD_DIM, 0:HEAD_DIM]

    kf = k_ref[...].astype(F32).reshape(nb, blk, HEAD_DIM)
    km = jnp.sum(kf, axis=1) * (1.0 / blk)
    km_hi = km.astype(BF16)
    km_lo = (km - km_hi.astype(F32)).astype(BF16)

    kaug[:, 0:HEAD_DIM] = k_ref[...]
    lane = lax.broadcasted_iota(I32, (blk, LANES), 1)
    cc = lax.broadcasted_iota(I32, (blk, LANES), 0).astype(F32)
    fb = F_BIAS - HEAD_DIM

    common = jnp.where((lane == fb) | (lane == fb + 1), 1.0, jnp.where(lane == fb + 2, slope * cc, 0.0))
    for j in range(nb):
        feat = jnp.where(lane == j, 1.0, jnp.where(lane == fb + 3, slope * float(j * blk), common))
        kaug[j * blk:(j + 1) * blk, HEAD_DIM:MOBA_AUG] = feat.astype(BF16)
        vt[j] = lax.dot_general(eye_hd, v_ref[j * blk:(j + 1) * blk, :], _NT,
                                preferred_element_type=F32).astype(BF16)

    q = q_ref[...]
    qaug[0:HEAD_DIM, :] = lax.dot_general(eye_hd, q, _NT, preferred_element_type=F32).astype(BF16)

    gate = (lax.dot_general(km_hi, q, _NT, preferred_element_type=F32)
            + lax.dot_general(km_lo, q, _NT, preferred_element_type=F32))
    sub = lax.broadcasted_iota(I32, (nb, seq), 0)
    tpos = lax.broadcasted_iota(I32, (nb, seq), 1)
    qblk = lax.shift_right_logical(tpos, blk.bit_length() - 1)
    rank = jnp.zeros((nb, seq), F32)
    for jp in range(nb):
        row = gate[jp:jp + 1, :]
        beats = (row > gate) | ((row == gate) & (sub > jp))
        rank = rank + jnp.where(beats & (qblk > jp), 1.0, 0.0)
    allowed = ((sub < qblk) & (rank < MOBA_TOPK)) | (sub == qblk)
    qaug[F_MASK:F_MASK + BF16_ROWS, :] = jnp.where(allowed, 0.0, NEG).astype(BF16)

    rq = (tpos & (blk - 1)).astype(F32)
    bias = jnp.where(sub == 0, -slope * rq, 0.0)
    bias = jnp.where(sub == 1, -slope * (qblk * blk).astype(F32), bias)
    bias = jnp.where((sub == 2) | (sub == 3), 1.0, bias)
    qaug[F_BIAS:F_BIAS + BF16_ROWS, :] = bias.astype(BF16)
    qaug[F_BIAS + BF16_ROWS:MOBA_AUG, :] = jnp.zeros((MOBA_AUG - F_BIAS - BF16_ROWS, seq), BF16)


def _moba_tile(n, kaug, vt, qaug, ident, o_ref):
    blk = MOBA_BLOCK
    qa = qaug[:, n * blk:(n + 1) * blk]
    kc = lax.broadcasted_iota(I32, (blk, blk), 0)
    qr = lax.broadcasted_iota(I32, (blk, blk), 1)
    ss = [jnp.dot(kaug[j * blk:(j + 1) * blk, :], qa, preferred_element_type=F32) for j in range(n + 1)]
    ss[n] = jnp.where(kc <= qr, ss[n], NEG)
    m = jnp.max(ss[0], axis=0, keepdims=True)
    for j in range(1, n + 1):
        m = jnp.maximum(m, jnp.max(ss[j], axis=0, keepdims=True))
    l = jnp.zeros((1, blk), F32)
    acc = jnp.zeros((HEAD_DIM, blk), F32)
    for j in range(n + 1):
        p = jnp.exp(ss[j] - m)
        l = l + jnp.sum(p, axis=0, keepdims=True)
        acc = acc + jnp.dot(vt[j], p.astype(BF16), preferred_element_type=F32)
    out_t = (acc / l).astype(BF16)
    o_ref[n * blk:(n + 1) * blk, :] = lax.dot_general(ident[...], out_t, _NT,
                                                      preferred_element_type=F32).astype(o_ref.dtype)


def _moba_kernel(slopes_ref, q_ref, k_ref, v_ref, o_ref, kaug, vt, qaug, ident):
    step = pl.program_id(2)
    nq = k_ref.shape[0] // MOBA_BLOCK

    @pl.when(step == 0)
    def _():
        _moba_prepare(slopes_ref[pl.program_id(1)], q_ref, k_ref, v_ref, kaug, vt, qaug, ident)

    for s in range(nq // MOBA_TILES_PER_STEP):
        @pl.when(step == s)
        def _(s=s):
            for n in _moba_step_tiles(s, nq):
                _moba_tile(n, kaug, vt, qaug, ident, o_ref)


def _moba_step_tiles(s, nq):
    pairs = MOBA_TILES_PER_STEP // 2
    out = []
    for k in range(s * pairs, (s + 1) * pairs):
        out += [k, nq - 1 - k]
    return out


def _moba(u, slopes, batch, seq):
    t = u.shape[0]
    nq = seq // MOBA_BLOCK
    assert nq == BF16_ROWS
    qc = (3 * D_CONV) // HEAD_DIM
    kc = qc + N_HEADS
    vc = kc + N_HEADS
    whole = lambda c0: pl.BlockSpec((seq, HEAD_DIM), lambda b, h, i, s: (b, c0 + h))
    return pl.pallas_call(
        _moba_kernel,
        grid_spec=pltpu.PrefetchScalarGridSpec(
            num_scalar_prefetch=1,
            grid=(batch, N_HEADS, nq // MOBA_TILES_PER_STEP),
            in_specs=[whole(qc), whole(kc), whole(vc)],
            out_specs=pl.BlockSpec((seq, HEAD_DIM), lambda b, h, i, s: (b, h)),
            scratch_shapes=[
                pltpu.VMEM((seq, MOBA_AUG), BF16),
                pltpu.VMEM((nq, HEAD_DIM, MOBA_BLOCK), BF16),
                pltpu.VMEM((MOBA_AUG, seq), BF16),
                pltpu.VMEM((MOBA_BLOCK, MOBA_BLOCK), BF16),
            ],
        ),
        out_shape=jax.ShapeDtypeStruct((t, D_ATT), BF16),
        compiler_params=_cparams(("arbitrary", "arbitrary", "arbitrary")),
        name="moba",
    )(slopes, u, u, u)


def _first_max(v, rows):
    top = jnp.max(v, axis=0, keepdims=True)
    idx = jnp.min(jnp.where(v == top, rows, float(v.shape[0])), axis=0, keepdims=True)
    return top, idx


def _router(lg, tri_ref, carry_ref):
    tm = lg.shape[1]
    rows = lax.broadcasted_iota(I32, (SUBLANES, tm), 0).astype(F32)
    grp_lg = jnp.where(rows < N_GROUPS, lg[0:SUBLANES, :], NEG)
    ex = jnp.exp(grp_lg - jnp.max(grp_lg, axis=0, keepdims=True))
    pg = ex / jnp.sum(ex, axis=0, keepdims=True)
    p_grp, grp = _first_max(pg, rows)
    le = jnp.zeros((EXPERTS_PER_GROUP, tm), F32)
    for g in range(N_GROUPS):
        r0 = ROUTER_E0 + g * EXPERTS_PER_GROUP
        le = jnp.where(grp == g, lg[r0:r0 + EXPERTS_PER_GROUP, :], le)
    ex = jnp.exp(le - jnp.max(le, axis=0, keepdims=True))
    q = ex / jnp.sum(ex, axis=0, keepdims=True)
    q1, i1 = _first_max(q, rows)
    q2, i2 = _first_max(jnp.where(rows == i1, -1.0, q), rows)
    den = q1 + q2
    g0 = p_grp * q1 / den
    g1 = p_grp * q2 / den
    e0 = grp * EXPERTS_PER_GROUP + i1
    e1 = grp * EXPERTS_PER_GROUP + i2

    erow = lax.broadcasted_iota(I32, (N_EXPERTS, tm), 0).astype(F32)
    is0 = erow == e0
    is1 = erow == e1
    onehot = jnp.where(is0 | is1, 1.0, 0.0)
    before = jnp.dot(onehot.astype(BF16), tri_ref[...], preferred_element_type=F32) + carry_ref[:, 0:1]
    r0 = jnp.sum(jnp.where(is0, before, 0.0), axis=0, keepdims=True)
    r1 = jnp.sum(jnp.where(is1, before, 0.0), axis=0, keepdims=True)
    carry_ref[...] = carry_ref[...] + jnp.sum(onehot, axis=1, keepdims=True)
    return e0, e1, g0, g1, r0, r1


def _out_proj_kernel(x_ref, ya_ref, yb_ref, yc_ref, w_ref, g_ref, wrt_ref, brt_ref,
                     x1_ref, h2_ref, ri_ref, rf_ref, cnt_ref, tri_ref, carry_ref):
    tm = OUT_TM

    @pl.when(pl.program_id(0) == 0)
    def _():
        a = lax.broadcasted_iota(I32, (tm, tm), 0)
        b = lax.broadcasted_iota(I32, (tm, tm), 1)
        tri_ref[...] = jnp.where(a < b, 1.0, 0.0).astype(BF16)
        carry_ref[...] = jnp.zeros(carry_ref.shape, F32)

    acc = x_ref[...]
    acc = acc + jnp.dot(ya_ref[...], w_ref[0:D_CONV, :], preferred_element_type=F32)
    acc = acc + jnp.dot(yb_ref[...], w_ref[D_CONV:D_CONV + D_ATT, :], preferred_element_type=F32)
    acc = acc + jnp.dot(yc_ref[...], w_ref[D_CONV + D_ATT:D_MIX, :], preferred_element_type=F32)
    x1_ref[...] = acc
    ms = jnp.mean(acc * acc, axis=-1, keepdims=True)
    h2 = acc * lax.rsqrt(ms + EPS) * g_ref[...]
    h2_ref[...] = h2

    lg = lax.dot_general(wrt_ref[...], h2.astype(BF16), _NT, preferred_element_type=F32) + brt_ref[...]
    e0, e1, g0, g1, r0, r1 = _router(lg, tri_ref, carry_ref)
    row = lax.broadcasted_iota(I32, (SUBLANES, tm), 0)
    ints = jnp.where(row == 0, e0, jnp.where(row == 1, e1, jnp.where(row == 2, r0, jnp.where(row == 3, r1, 0.0))))
    ri_ref[...] = ints.astype(I32)
    rf_ref[...] = jnp.where(row == 0, g0, jnp.where(row == 1, g1, 0.0))
    cnt_ref[...] = carry_ref[...]


def _out_proj(x, ya, yb, yc, w_bf, g, wrt_bf, brt):
    t = x.shape[0]
    rows = lambda n: pl.BlockSpec((OUT_TM, n), lambda i: (i, 0))
    whole = lambda a, b: pl.BlockSpec((a, b), lambda i: (0, 0))
    cols = pl.BlockSpec((SUBLANES, OUT_TM), lambda i: (0, i))
    return pl.pallas_call(
        _out_proj_kernel,
        grid=(t // OUT_TM,),
        in_specs=[rows(D_MODEL), rows(D_CONV), rows(D_ATT), rows(D_CFM),
                  pl.BlockSpec((D_MIX, D_MODEL), lambda i: (0, 0), pipeline_mode=pl.Buffered(1)),
                  whole(1, D_MODEL), whole(ROUTER_ROWS, D_MODEL), whole(ROUTER_ROWS, 1)],
        out_specs=[rows(D_MODEL), rows(D_MODEL), cols, cols, whole(N_EXPERTS, LANES)],
        out_shape=[jax.ShapeDtypeStruct((t, D_MODEL), F32), jax.ShapeDtypeStruct((t, D_MODEL), F32),
                   jax.ShapeDtypeStruct((SUBLANES, t), I32), jax.ShapeDtypeStruct((SUBLANES, t), F32),
                   jax.ShapeDtypeStruct((N_EXPERTS, LANES), F32)],
        scratch_shapes=[pltpu.VMEM((OUT_TM, OUT_TM), BF16), pltpu.VMEM((N_EXPERTS, LANES), F32)],
        compiler_params=_cparams(("arbitrary",)),
        name="out_proj",
    )(x, ya, yb, yc, w_bf, g.reshape(1, D_MODEL), wrt_bf, brt)


def _dispatch_kernel(pad_row_ref, pad_n_ref, nb_ref, dest_ref, h2_ref, xs_hbm, stage, sem, pad_sem):
    i = pl.program_id(0)
    n = pl.num_programs(0)
    slot = lax.rem(i, 2)

    def row_copy(s_idx, r, dst_row, s):
        return pltpu.make_async_copy(stage.at[s_idx, pl.ds(r, 1)], xs_hbm.at[pl.ds(dst_row, 1)], s)

    def tile_wait(s_idx):
        for _ in range(TOPK_IN_GROUP):
            pltpu.make_async_copy(stage.at[s_idx], xs_hbm.at[pl.ds(0, TOK_TM)], sem.at[s_idx]).wait()

    def for_pad_rows(fn):
        def per_expert(e, carry):
            def one(k, c):
                fn(row_copy(0, 0, pad_row_ref[e] + k, pad_sem.at[0]))
                return c
            lax.fori_loop(0, pad_n_ref[e], one, 0)
            return carry
        lax.fori_loop(0, N_EXPERTS, per_expert, 0)

    def for_tail_blocks(fn):
        def one(b, c):
            start = pl.multiple_of(b * FFN_TM, FFN_TM)
            fn(pltpu.make_async_copy(stage.at[0, pl.ds(0, FFN_TM)], xs_hbm.at[pl.ds(start, FFN_TM)],
                                     pad_sem.at[0]))
            return c
        lax.fori_loop(nb_ref[0], xs_hbm.shape[0] // FFN_TM, one, 0)

    @pl.when(i == 1)
    def _():
        for_pad_rows(lambda cp: cp.wait())
        for_tail_blocks(lambda cp: cp.wait())

    stage[slot] = h2_ref[...]

    @pl.when(i == 0)
    def _():
        for_pad_rows(lambda cp: cp.start())
        for_tail_blocks(lambda cp: cp.start())

    def issue(r, carry):
        row_copy(slot, r, dest_ref[0, 0, r], sem.at[slot]).start(priority=0)
        row_copy(slot, r, dest_ref[0, 0, TOK_TM + r], sem.at[slot]).start(priority=1)
        return carry

    lax.fori_loop(0, TOK_TM, issue, 0, unroll=8)

    @pl.when(i > 0)
    def _():
        tile_wait(1 - slot)

    @pl.when(i == n - 1)
    def _():
        tile_wait(slot)


def _dispatch(h2, dest3, pad_row, pad_n, n_used, n_rows):
    nt = dest3.shape[0]
    assert nt >= 2 and TOK_TM >= FFN_TM
    return pl.pallas_call(
        _dispatch_kernel,
        grid_spec=pltpu.PrefetchScalarGridSpec(
            num_scalar_prefetch=3,
            grid=(nt,),
            in_specs=[
                pl.BlockSpec((1, 1, 2 * TOK_TM), lambda i, a, b, c: (i, 0, 0), memory_space=pltpu.SMEM),
                pl.BlockSpec((TOK_TM, D_MODEL), lambda i, a, b, c: (i, 0)),
            ],
            out_specs=pl.BlockSpec(memory_space=pl.ANY),
            scratch_shapes=[pltpu.VMEM((2, TOK_TM, D_MODEL), F32),
                            pltpu.SemaphoreType.DMA((2,)), pltpu.SemaphoreType.DMA((1,))],
        ),
        out_shape=jax.ShapeDtypeStruct((n_rows, D_MODEL), F32),
        compiler_params=_cparams(("arbitrary",)),
        name="dispatch",
    )(pad_row, pad_n, n_used, dest3, h2)


def _ffn_kernel(be_ref, nb_ref, nxt_ref, xs_ref, wg_hbm, wu_hbm, wd_hbm, y_ref,
                stg_g, stg_u, stg_d, wgb, wub, wdb, sem, *, layer):
    i = pl.program_id(0)
    nb = nb_ref[0]

    def weight_copies(e):
        return (pltpu.make_async_copy(wg_hbm.at[layer, e], stg_g, sem.at[0]),
                pltpu.make_async_copy(wu_hbm.at[layer, e], stg_u, sem.at[1]),
                pltpu.make_async_copy(wd_hbm.at[layer, e], stg_d, sem.at[2]))

    @pl.when(i == 0)
    def _():
        for cp in weight_copies(be_ref[0]):
            cp.start()

    @pl.when(i < nb)
    def _():
        e = be_ref[i]
        prev = be_ref[jnp.maximum(i - 1, 0)]

        @pl.when((i == 0) | (e != prev))
        def _():
            for cp in weight_copies(e):
                cp.wait()
            wgb[...] = stg_g[...].astype(BF16)
            wub[...] = stg_u[...].astype(BF16)
            wdb[...] = stg_d[...].astype(BF16)
            nxt = nxt_ref[e]

            @pl.when(nxt < N_EXPERTS)
            def _():
                for cp in weight_copies(nxt):
                    cp.start()

        xb = xs_ref[...].astype(BF16)
        g = jnp.dot(xb, wgb[...], preferred_element_type=F32)
        up = jnp.dot(xb, wub[...], preferred_element_type=F32)
        hid = (g * jax.nn.sigmoid(g) * up).astype(BF16)
        y_ref[...] = jnp.dot(hid, wdb[...], preferred_element_type=F32)

    @pl.when(i >= nb)
    def _():
        y_ref[...] = jnp.zeros(y_ref.shape, y_ref.dtype)


def _ffn(xs, blk_expert, n_used, next_used, w_gate, w_up, w_down, layer):
    n_rows = xs.shape[0]
    n_blocks = n_rows // FFN_TM
    hbm = pl.BlockSpec(memory_space=pl.ANY)
    return pl.pallas_call(
        functools.partial(_ffn_kernel, layer=layer),
        grid_spec=pltpu.PrefetchScalarGridSpec(
            num_scalar_prefetch=3,
            grid=(n_blocks,),
            in_specs=[
                pl.BlockSpec((FFN_TM, D_MODEL), lambda i, be, nb, nx: (jnp.minimum(i, nb[0] - 1), 0)),
                hbm, hbm, hbm,
            ],
            out_specs=pl.BlockSpec((FFN_TM, D_MODEL), lambda i, be, nb, nx: (i, 0)),
            scratch_shapes=[
                pltpu.VMEM((D_MODEL, D_EXPERT), F32),
                pltpu.VMEM((D_MODEL, D_EXPERT), F32),
                pltpu.VMEM((D_EXPERT, D_MODEL), F32),
                pltpu.VMEM((D_MODEL, D_EXPERT), BF16),
                pltpu.VMEM((D_MODEL, D_EXPERT), BF16),
                pltpu.VMEM((D_EXPERT, D_MODEL), BF16),
                pltpu.SemaphoreType.DMA((3,)),
            ],
        ),
        out_shape=jax.ShapeDtypeStruct((n_rows, D_MODEL), F32),
        compiler_params=_cparams(("arbitrary",)),
        name="ffn",
    )(blk_expert, n_used, next_used, xs, w_gate, w_up, w_down)


def _gather_rows(idx_ref, n, src_hbm, dst, sem):
    def body(k, carry):
        for parity in range(2):
            r = 2 * k + parity
            pltpu.make_async_copy(src_hbm.at[pl.ds(idx_ref[0, 0, r], 1)], dst.at[pl.ds(r, 1)],
                                  sem).start(priority=parity)
        return carry
    lax.fori_loop(0, n // 2, body, 0, unroll=4)


def _wait_rows(n, src_hbm, dst, sem):
    pltpu.make_async_copy(src_hbm.at[pl.ds(0, n)], dst, sem).wait()


def _combine_kernel(pos_ref, posn_ref, x1_ref, gt_ref, y_hbm, g_ref, o_ref, ybuf, sem, *, final_norm):
    i = pl.program_id(0)
    n = pl.num_programs(0)
    slot = lax.rem(i, 2)
    tm = TOK_TM

    @pl.when(i == 0)
    def _():
        _gather_rows(pos_ref, 2 * tm, y_hbm, ybuf.at[0], sem.at[0])

    @pl.when(i + 1 < n)
    def _():
        _gather_rows(posn_ref, 2 * tm, y_hbm, ybuf.at[1 - slot], sem.at[1 - slot])

    _wait_rows(2 * tm, y_hbm, ybuf.at[slot], sem.at[slot])
    out = x1_ref[...] + (ybuf[slot, 0:tm, :] * gt_ref[:, 0:1] + ybuf[slot, tm:2 * tm, :] * gt_ref[:, 1:2])
    if final_norm:
        ms = jnp.mean(out * out, axis=-1, keepdims=True)
        out = out * lax.rsqrt(ms + EPS) * g_ref[...]
    o_ref[...] = out


def _combine(x1, y, dest3, gates_t, g, final_norm):
    t = x1.shape[0]
    nt = t // TOK_TM
    return pl.pallas_call(
        functools.partial(_combine_kernel, final_norm=final_norm),
        grid=(nt,),
        in_specs=[
            pl.BlockSpec((1, 1, 2 * TOK_TM), lambda i: (i, 0, 0), memory_space=pltpu.SMEM),
            pl.BlockSpec((1, 1, 2 * TOK_TM), lambda i: (jnp.minimum(i + 1, nt - 1), 0, 0),
                         memory_space=pltpu.SMEM),
            pl.BlockSpec((TOK_TM, D_MODEL), lambda i: (i, 0)),
            pl.BlockSpec((TOK_TM, TOPK_IN_GROUP), lambda i: (i, 0)),
            pl.BlockSpec(memory_space=pl.ANY),
            pl.BlockSpec((1, D_MODEL), lambda i: (0, 0)),
        ],
        out_specs=pl.BlockSpec((TOK_TM, D_MODEL), lambda i: (i, 0)),
        out_shape=jax.ShapeDtypeStruct((t, D_MODEL), F32),
        scratch_shapes=[pltpu.VMEM((2, 2 * TOK_TM, D_MODEL), F32), pltpu.SemaphoreType.DMA((2,))],
        compiler_params=_cparams(("arbitrary",)),
        name="combine",
    )(dest3, dest3, x1, gates_t, y, g.reshape(1, D_MODEL))


def _row_tables(ri, counts_f, t):
    counts = counts_f[:, 0].astype(I32)
    padded = (counts + FFN_TM - 1) // FFN_TM * FFN_TM
    end_p = jnp.cumsum(padded)
    start_p = end_p - padded
    n_rows = t * TOPK_IN_GROUP + N_EXPERTS * FFN_TM
    blk_start = jnp.arange(n_rows // FFN_TM, dtype=I32) * FFN_TM
    blk_expert = jnp.minimum(jnp.sum((end_p[None, :] <= blk_start[:, None]).astype(I32), axis=1), N_EXPERTS - 1)
    n_used = (end_p[-1] // FFN_TM).astype(I32).reshape(1)
    ids = jnp.arange(N_EXPERTS, dtype=I32)
    later = (ids[None, :] > ids[:, None]) & (counts[None, :] > 0)
    next_used = jnp.min(jnp.where(later, ids[None, :], N_EXPERTS), axis=1).astype(I32)
    expert = ri[0:TOPK_IN_GROUP]
    rank = ri[TOPK_IN_GROUP:2 * TOPK_IN_GROUP]
    first = jnp.sum(jnp.where(expert[:, :, None] == jnp.arange(N_EXPERTS, dtype=I32), start_p, 0), axis=-1)
    dest = rank + first
    nt = t // TOK_TM
    dest3 = dest.reshape(TOPK_IN_GROUP, nt, TOK_TM).transpose(1, 0, 2).reshape(nt, 1, TOPK_IN_GROUP * TOK_TM)
    return (blk_expert, n_used, next_used, dest3, (start_p + counts).astype(I32), (padded - counts).astype(I32),
            n_rows)


def _router_weights(rg_w, rg_b, re_w, re_b):
    wrt = jnp.zeros((ROUTER_ROWS, D_MODEL), F32)
    wrt = wrt.at[0:N_GROUPS].set(rg_w.T).at[ROUTER_E0:ROUTER_E0 + N_EXPERTS].set(re_w.T)
    brt = jnp.zeros((ROUTER_ROWS,), F32)
    brt = brt.at[0:N_GROUPS].set(rg_b).at[ROUTER_E0:ROUTER_E0 + N_EXPERTS].set(re_b)
    return wrt.astype(BF16), brt.reshape(ROUTER_ROWS, 1)


def kernel(x, norm1_g, w_in, conv_a_w, conv_c_w, conv_c_b, ln_c_g, ln_c_b, w_out, norm2_g,
           router_g_w, router_g_b, router_e_w, router_e_b, w_gate, w_up, w_down, final_g):
    batch, seq, d = x.shape
    t = batch * seq
    depth = w_in.shape[0]
    xt = x.reshape(t, d)
    slopes = jnp.exp2(-8.0 * jnp.arange(1, N_HEADS + 1, dtype=F32) / N_HEADS)
    for l in range(depth):
        u = _in_proj(xt, norm1_g[l], w_in[l].astype(BF16))
        ya, yc = _mixers(u, conv_a_w[l], conv_c_w[l], conv_c_b[l], ln_c_g[l], ln_c_b[l], batch, seq)
        yb = _moba(u, slopes, batch, seq)
        wrt, brt = _router_weights(router_g_w[l], router_g_b[l], router_e_w[l], router_e_b[l])
        x1, h2, ri, rf, counts = _out_proj(xt, ya, yb, yc, w_out[l].astype(BF16), norm2_g[l], wrt, brt)
        blk_expert, n_used, next_used, dest3, pad_row, pad_n, n_rows = _row_tables(ri, counts, t)
        xs = _dispatch(h2, dest3, pad_row, pad_n, n_used, n_rows)
        y = _ffn(xs, blk_expert, n_used, next_used, w_gate, w_up, w_down, l)
        xt = _combine(x1, y, dest3, rf[0:TOPK_IN_GROUP].T, final_g, final_norm=(l == depth - 1))
    return xt.reshape(batch, seq, d)
```

```python
import functools

import jax
import jax.numpy as jnp
from jax import lax
from jax.experimental import pallas as pl
from jax.experimental.pallas import tpu as pltpu

D_MODEL = 2048
D_CONV = 512
D_ATT = 1024
N_HEADS = 8
HEAD_DIM = 128
D_CFM = 512
D_MIX = D_CONV + D_ATT + D_CFM
D_IN = 3 * D_CONV + 3 * D_ATT + 2 * D_CFM
SHORT_CONV_W = 3
CFM_CONV_W = 31
MOBA_BLOCK = 256
MOBA_TOPK = 3
N_GROUPS = 4
EXPERTS_PER_GROUP = 8
N_EXPERTS = N_GROUPS * EXPERTS_PER_GROUP
TOPK_IN_GROUP = 2
D_EXPERT = 512
EPS = 1e-6
NEG = -1e30

LANES = 128
SUBLANES = 8
BF16_ROWS = 16
VMEM_LIMIT = 56 * 1024 * 1024

IN_TM = 512
IN_TN = D_IN // 2
MIX_TS = 512
HALO_A = 8
HALO_C = 32
MOBA_AUG = 2 * HEAD_DIM
MOBA_TILES_PER_STEP = 16
OUT_TM = 512
ROUTER_ROWS = LANES
ROUTER_E0 = SUBLANES
FFN_TM = 256
TOK_TM = 512

BF16 = jnp.bfloat16
F32 = jnp.float32
I32 = jnp.int32


def _cparams(sem):
    return pltpu.CompilerParams(dimension_semantics=sem, vmem_limit_bytes=VMEM_LIMIT)


def _in_proj_kernel(x_ref, g_ref, w_ref, o_ref, hn_ref):
    j = pl.program_id(1)

    @pl.when(j == 0)
    def _():
        x = x_ref[...]
        ms = jnp.mean(x * x, axis=-1, keepdims=True)
        hn_ref[...] = (x * lax.rsqrt(ms + EPS) * g_ref[...]).astype(BF16)

    col = j * IN_TN + lax.broadcasted_iota(I32, (1, IN_TN), 1)
    is_q = (col >= 3 * D_CONV) & (col < 3 * D_CONV + D_ATT)
    col_scale = jnp.where(is_q, HEAD_DIM ** -0.5, 1.0).astype(F32)
    acc = jnp.dot(hn_ref[...], w_ref[...], preferred_element_type=F32)
    o_ref[...] = (acc * col_scale).astype(o_ref.dtype)


def _in_proj(x, g, w_bf):
    t = x.shape[0]
    return pl.pallas_call(
        _in_proj_kernel,
        grid=(t // IN_TM, D_IN // IN_TN),
        in_specs=[
            pl.BlockSpec((IN_TM, D_MODEL), lambda i, j: (i, 0)),
            pl.BlockSpec((1, D_MODEL), lambda i, j: (0, 0)),
            pl.BlockSpec((D_MODEL, IN_TN), lambda i, j: (0, j)),
        ],
        out_specs=pl.BlockSpec((IN_TM, IN_TN), lambda i, j: (i, j)),
        out_shape=jax.ShapeDtypeStruct((t, D_IN), BF16),
        scratch_shapes=[pltpu.VMEM((IN_TM, D_MODEL), BF16)],
        compiler_params=_cparams(("arbitrary", "arbitrary")),
        name="in_proj",
    )(x, g.reshape(1, D_MODEL), w_bf)


def _mix_kernel(bg_ref, cg_ref, xv_ref, a_ref, gt_ref, wa_ref, wc_ref, bc_ref, lg_ref, lb_ref,
                ya_ref, yc_ref, bufa, bufc, shc):
    s = pl.program_id(1)
    ts = MIX_TS

    @pl.when(s == 0)
    def _():
        bufa[0:HALO_A, :] = jnp.zeros((HALO_A, D_CONV), F32)
        bufc[0:HALO_C, :] = jnp.zeros((HALO_C, D_CFM), F32)

    @pl.when(s > 0)
    def _():
        bufa[0:HALO_A, :] = bufa[ts:ts + HALO_A, :]
        bufc[0:HALO_C, :] = bufc[ts:ts + HALO_C, :]

    bufa[HALO_A:HALO_A + ts, :] = cg_ref[...].astype(F32) * xv_ref[...].astype(F32)
    acc = jnp.zeros((ts, D_CONV), F32)
    for j in range(SHORT_CONV_W):
        off = HALO_A - (SHORT_CONV_W - 1) + j
        acc = acc + wa_ref[j:j + 1, :] * bufa[off:off + ts, :]
    ya_ref[...] = (bg_ref[...].astype(F32) * acc).astype(ya_ref.dtype)

    a = a_ref[...].astype(F32)
    gate = gt_ref[...].astype(F32)
    bufc[HALO_C:HALO_C + ts, :] = a * jax.nn.sigmoid(gate)
    span = ts + HALO_C - SUBLANES
    for r in range(1, SUBLANES):
        shc[r - 1, :, :] = bufc[r:r + span, :]
    acc = jnp.zeros((ts, D_CFM), F32) + bc_ref[...]
    for j in range(CFM_CONV_W):
        off = HALO_C - (CFM_CONV_W - 1) + j
        a, r = divmod(off, SUBLANES)
        win = bufc[off:off + ts, :] if r == 0 else shc[r - 1, a * SUBLANES:a * SUBLANES + ts, :]
        acc = acc + wc_ref[j:j + 1, :] * win
    mu = jnp.mean(acc, axis=-1, keepdims=True)
    cen = acc - mu
    var = jnp.mean(cen * cen, axis=-1, keepdims=True)
    hn = cen * lax.rsqrt(var + EPS) * lg_ref[...] + lb_ref[...]
    yc_ref[...] = (hn * jax.nn.sigmoid(hn)).astype(yc_ref.dtype)


def _mixers(u, wa, wc, bc, lg, lb, batch, seq):
    t = u.shape[0]
    nst = seq // MIX_TS
    row = lambda b, s: b * nst + s
    ublk = lambda c: pl.BlockSpec((MIX_TS, D_CONV), lambda b, s, c=c: (row(b, s), c))
    small = lambda r: pl.BlockSpec((r, D_CONV), lambda b, s: (0, 0))
    c0 = (3 * D_CONV + 3 * D_ATT) // D_CFM
    return pl.pallas_call(
        _mix_kernel,
        grid=(batch, nst),
        in_specs=[ublk(0), ublk(1), ublk(2), ublk(c0), ublk(c0 + 1),
                  small(SHORT_CONV_W), small(CFM_CONV_W), small(1), small(1), small(1)],
        out_specs=[pl.BlockSpec((MIX_TS, D_CONV), lambda b, s: (row(b, s), 0)),
                   pl.BlockSpec((MIX_TS, D_CFM), lambda b, s: (row(b, s), 0))],
        out_shape=[jax.ShapeDtypeStruct((t, D_CONV), BF16), jax.ShapeDtypeStruct((t, D_CFM), BF16)],
        scratch_shapes=[pltpu.VMEM((HALO_A + MIX_TS, D_CONV), F32),
                        pltpu.VMEM((HALO_C + MIX_TS, D_CFM), F32),
                        pltpu.VMEM((SUBLANES - 1, MIX_TS + HALO_C - SUBLANES, D_CFM), F32)],
        compiler_params=_cparams(("arbitrary", "arbitrary")),
        name="mixers",
    )(u, u, u, u, u, wa, wc, bc.reshape(1, D_CFM), lg.reshape(1, D_CFM), lb.reshape(1, D_CFM))


_NT = (((1,), (1,)), ((), ()))

F_MASK = HEAD_DIM
F_BIAS = HEAD_DIM + BF16_ROWS


def _moba_prepare(slope, q_ref, k_ref, v_ref, kaug, vt, qaug, ident):
    blk = MOBA_BLOCK
    seq = k_ref.shape[0]
    nb = seq // blk
    ri = lax.broadcasted_iota(I32, (blk, blk), 0)
    ci = lax.broadcasted_iota(I32, (blk, blk), 1)
    ident[...] = jnp.where(ri == ci, 1.0, 0.0).astype(BF16)
    eye_hd = ident[0:HEAD_DIM, 0:HEAD_DIM]

    kf = k_ref[...].astype(F32).reshape(nb, blk, HEAD_DIM)
    km = jnp.sum(kf, axis=1) * (1.0 / blk)
    km_hi = km.astype(BF16)
    km_lo = (km - km_hi.astype(F32)).astype(BF16)

    kaug[:, 0:HEAD_DIM] = k_ref[...]
    lane = lax.broadcasted_iota(I32, (blk, LANES), 1)
    cc = lax.broadcasted_iota(I32, (blk, LANES), 0).astype(F32)
    fb = F_BIAS - HEAD_DIM

    common = jnp.where((lane == fb) | (lane == fb + 1), 1.0, jnp.where(lane == fb + 2, slope * cc, 0.0))
    for j in range(nb):
        feat = jnp.where(lane == j, 1.0, jnp.where(lane == fb + 3, slope * float(j * blk), common))
        kaug[j * blk:(j + 1) * blk, HEAD_DIM:MOBA_AUG] = feat.astype(BF16)
        vt[j] = lax.dot_general(eye_hd, v_ref[j * blk:(j + 1) * blk, :], _NT,
                                preferred_element_type=F32).astype(BF16)

    q = q_ref[...]
    qaug[0:HEAD_DIM, :] = lax.dot_general(eye_hd, q, _NT, preferred_element_type=F32).astype(BF16)

    gate = (lax.dot_general(km_hi, q, _NT, preferred_element_type=F32)
            + lax.dot_general(km_lo, q, _NT, preferred_element_type=F32))
    sub = lax.broadcasted_iota(I32, (nb, seq), 0)
    tpos = lax.broadcasted_iota(I32, (nb, seq), 1)
    qblk = lax.shift_right_logical(tpos, blk.bit_length() - 1)
    rank = jnp.zeros((nb, seq), F32)
    for jp in range(nb):
        row = gate[jp:jp + 1, :]
        beats = (row > gate) | ((row == gate) & (sub > jp))
        rank = rank + jnp.where(beats & (qblk > jp), 1.0, 0.0)
    allowed = ((sub < qblk) & (rank < MOBA_TOPK)) | (sub == qblk)
    qaug[F_MASK:F_MASK + BF16_ROWS, :] = jnp.where(allowed, 0.0, NEG).astype(BF16)

    rq = (tpos & (blk - 1)).astype(F32)
    bias = jnp.where(sub == 0, -slope * rq, 0.0)
    bias = jnp.where(sub == 1, -slope * (qblk * blk).astype(F32), bias)
    bias = jnp.where((sub == 2) | (sub == 3), 1.0, bias)
    qaug[F_BIAS:F_BIAS + BF16_ROWS, :] = bias.astype(BF16)
    qaug[F_BIAS + BF16_ROWS:MOBA_AUG, :] = jnp.zeros((MOBA_AUG - F_BIAS - BF16_ROWS, seq), BF16)


def _moba_tile(n, kaug, vt, qaug, ident, o_ref):
    blk = MOBA_BLOCK
    qa = qaug[:, n * blk:(n + 1) * blk]
    kc = lax.broadcasted_iota(I32, (blk, blk), 0)
    qr = lax.broadcasted_iota(I32, (blk, blk), 1)
    ss = [jnp.dot(kaug[j * blk:(j + 1) * blk, :], qa, preferred_element_type=F32) for j in range(n + 1)]
    ss[n] = jnp.where(kc <= qr, ss[n], NEG)
    m = jnp.max(ss[0], axis=0, keepdims=True)
    for j in range(1, n + 1):
        m = jnp.maximum(m, jnp.max(ss[j], axis=0, keepdims=True))
    l = jnp.zeros((1, blk), F32)
    acc = jnp.zeros((HEAD_DIM, blk), F32)
    for j in range(n + 1):
        p = jnp.exp(ss[j] - m)
        l = l + jnp.sum(p, axis=0, keepdims=True)
        acc = acc + jnp.dot(vt[j], p.astype(BF16), preferred_element_type=F32)
    out_t = (acc / l).astype(BF16)
    o_ref[n * blk:(n + 1) * blk, :] = lax.dot_general(ident[...], out_t, _NT,
                                                      preferred_element_type=F32).astype(o_ref.dtype)


def _moba_kernel(slopes_ref, q_ref, k_ref, v_ref, o_ref, kaug, vt, qaug, ident):
    step = pl.program_id(2)
    nq = k_ref.shape[0] // MOBA_BLOCK

    @pl.when(step == 0)
    def _():
        _moba_prepare(slopes_ref[pl.program_id(1)], q_ref, k_ref, v_ref, kaug, vt, qaug, ident)

    for s in range(nq // MOBA_TILES_PER_STEP):
        @pl.when(step == s)
        def _(s=s):
            for n in _moba_step_tiles(s, nq):
                _moba_tile(n, kaug, vt, qaug, ident, o_ref)


def _moba_step_tiles(s, nq):
    pairs = MOBA_TILES_PER_STEP // 2
    out = []
    for k in range(s * pairs, (s + 1) * pairs):
        out += [k, nq - 1 - k]
    return out


def _moba(u, slopes, batch, seq):
    t = u.shape[0]
    nq = seq // MOBA_BLOCK
    assert nq == BF16_ROWS
    qc = (3 * D_CONV) // HEAD_DIM
    kc = qc + N_HEADS
    vc = kc + N_HEADS
    whole = lambda c0: pl.BlockSpec((seq, HEAD_DIM), lambda b, h, i, s: (b, c0 + h))
    return pl.pallas_call(
        _moba_kernel,
        grid_spec=pltpu.PrefetchScalarGridSpec(
            num_scalar_prefetch=1,
            grid=(batch, N_HEADS, nq // MOBA_TILES_PER_STEP),
            in_specs=[whole(qc), whole(kc), whole(vc)],
            out_specs=pl.BlockSpec((seq, HEAD_DIM), lambda b, h, i, s: (b, h)),
            scratch_shapes=[
                pltpu.VMEM((seq, MOBA_AUG), BF16),
                pltpu.VMEM((nq, HEAD_DIM, MOBA_BLOCK), BF16),
                pltpu.VMEM((MOBA_AUG, seq), BF16),
                pltpu.VMEM((MOBA_BLOCK, MOBA_BLOCK), BF16),
            ],
        ),
        out_shape=jax.ShapeDtypeStruct((t, D_ATT), BF16),
        compiler_params=_cparams(("arbitrary", "arbitrary", "arbitrary")),
        name="moba",
    )(slopes, u, u, u)


def _first_max(v, rows):
    top = jnp.max(v, axis=0, keepdims=True)
    idx = jnp.min(jnp.where(v == top, rows, float(v.shape[0])), axis=0, keepdims=True)
    return top, idx


def _router(lg, tri_ref, carry_ref):
    tm = lg.shape[1]
    rows = lax.broadcasted_iota(I32, (SUBLANES, tm), 0).astype(F32)
    grp_lg = jnp.where(rows < N_GROUPS, lg[0:SUBLANES, :], NEG)
    ex = jnp.exp(grp_lg - jnp.max(grp_lg, axis=0, keepdims=True))
    pg = ex / jnp.sum(ex, axis=0, keepdims=True)
    p_grp, grp = _first_max(pg, rows)
    le = jnp.zeros((EXPERTS_PER_GROUP, tm), F32)
    for g in range(N_GROUPS):
        r0 = ROUTER_E0 + g * EXPERTS_PER_GROUP
        le = jnp.where(grp == g, lg[r0:r0 + EXPERTS_PER_GROUP, :], le)
    ex = jnp.exp(le - jnp.max(le, axis=0, keepdims=True))
    q = ex / jnp.sum(ex, axis=0, keepdims=True)
    q1, i1 = _first_max(q, rows)
    q2, i2 = _first_max(jnp.where(rows == i1, -1.0, q), rows)
    den = q1 + q2
    g0 = p_grp * q1 / den
    g1 = p_grp * q2 / den
    e0 = grp * EXPERTS_PER_GROUP + i1
    e1 = grp * EXPERTS_PER_GROUP + i2

    erow = lax.broadcasted_iota(I32, (N_EXPERTS, tm), 0).astype(F32)
    is0 = erow == e0
    is1 = erow == e1
    onehot = jnp.where(is0 | is1, 1.0, 0.0)
    before = jnp.dot(onehot.astype(BF16), tri_ref[...], preferred_element_type=F32) + carry_ref[:, 0:1]
    r0 = jnp.sum(jnp.where(is0, before, 0.0), axis=0, keepdims=True)
    r1 = jnp.sum(jnp.where(is1, before, 0.0), axis=0, keepdims=True)
    carry_ref[...] = carry_ref[...] + jnp.sum(onehot, axis=1, keepdims=True)
    return e0, e1, g0, g1, r0, r1


def _out_proj_kernel(x_ref, ya_ref, yb_ref, yc_ref, w_ref, g_ref, wrt_ref, brt_ref,
                     x1_ref, ri_ref, rf_ref, cnt_ref, tri_ref, carry_ref):
    tm = OUT_TM

    @pl.when(pl.program_id(0) == 0)
    def _():
        a = lax.broadcasted_iota(I32, (tm, tm), 0)
        b = lax.broadcasted_iota(I32, (tm, tm), 1)
        tri_ref[...] = jnp.where(a < b, 1.0, 0.0).astype(BF16)
        carry_ref[...] = jnp.zeros(carry_ref.shape, F32)

    acc = x_ref[...]
    acc = acc + jnp.dot(ya_ref[...], w_ref[0:D_CONV, :], preferred_element_type=F32)
    acc = acc + jnp.dot(yb_ref[...], w_ref[D_CONV:D_CONV + D_ATT, :], preferred_element_type=F32)
    acc = acc + jnp.dot(yc_ref[...], w_ref[D_CONV + D_ATT:D_MIX, :], preferred_element_type=F32)
    x1_ref[...] = acc
    ms = jnp.mean(acc * acc, axis=-1, keepdims=True)
    h2 = acc * lax.rsqrt(ms + EPS) * g_ref[...]

    lg = lax.dot_general(wrt_ref[...], h2.astype(BF16), _NT, preferred_element_type=F32) + brt_ref[...]
    e0, e1, g0, g1, r0, r1 = _router(lg, tri_ref, carry_ref)
    row = lax.broadcasted_iota(I32, (SUBLANES, tm), 0)
    ints = jnp.where(row == 0, e0, jnp.where(row == 1, e1, jnp.where(row == 2, r0, jnp.where(row == 3, r1, 0.0))))
    ri_ref[...] = ints.astype(I32)
    rf_ref[...] = jnp.where(row == 0, g0, jnp.where(row == 1, g1, 0.0))
    cnt_ref[...] = carry_ref[...]


def _out_proj(x, ya, yb, yc, w_bf, g, wrt_bf, brt):
    t = x.shape[0]
    rows = lambda n: pl.BlockSpec((OUT_TM, n), lambda i: (i, 0))
    whole = lambda a, b: pl.BlockSpec((a, b), lambda i: (0, 0))
    cols = pl.BlockSpec((SUBLANES, OUT_TM), lambda i: (0, i))
    return pl.pallas_call(
        _out_proj_kernel,
        grid=(t // OUT_TM,),
        in_specs=[rows(D_MODEL), rows(D_CONV), rows(D_ATT), rows(D_CFM),
                  pl.BlockSpec((D_MIX, D_MODEL), lambda i: (0, 0), pipeline_mode=pl.Buffered(1)),
                  whole(1, D_MODEL), whole(ROUTER_ROWS, D_MODEL), whole(ROUTER_ROWS, 1)],
        out_specs=[rows(D_MODEL), cols, cols, whole(N_EXPERTS, LANES)],
        out_shape=[jax.ShapeDtypeStruct((t, D_MODEL), F32),
                   jax.ShapeDtypeStruct((SUBLANES, t), I32), jax.ShapeDtypeStruct((SUBLANES, t), F32),
                   jax.ShapeDtypeStruct((N_EXPERTS, LANES), F32)],
        scratch_shapes=[pltpu.VMEM((OUT_TM, OUT_TM), BF16), pltpu.VMEM((N_EXPERTS, LANES), F32)],
        compiler_params=_cparams(("arbitrary",)),
        name="out_proj",
    )(x, ya, yb, yc, w_bf, g.reshape(1, D_MODEL), wrt_bf, brt)


def _dispatch_kernel(pad_row_ref, pad_n_ref, nb_ref, dest_ref, x1_ref, g_ref, xs_hbm, stage, sem, pad_sem):
    i = pl.program_id(0)
    n = pl.num_programs(0)
    slot = lax.rem(i, 2)

    def row_copy(s_idx, r, dst_row, s):
        return pltpu.make_async_copy(stage.at[s_idx, pl.ds(r, 1)], xs_hbm.at[pl.ds(dst_row, 1)], s)

    def tile_wait(s_idx):
        for _ in range(TOPK_IN_GROUP):
            pltpu.make_async_copy(stage.at[s_idx], xs_hbm.at[pl.ds(0, TOK_TM)], sem.at[s_idx]).wait()

    def for_pad_rows(fn):
        def per_expert(e, carry):
            def one(k, c):
                fn(row_copy(0, 0, pad_row_ref[e] + k, pad_sem.at[0]))
                return c
            lax.fori_loop(0, pad_n_ref[e], one, 0)
            return carry
        lax.fori_loop(0, N_EXPERTS, per_expert, 0)

    def for_tail_blocks(fn):
        def one(b, c):
            start = pl.multiple_of(b * FFN_TM, FFN_TM)
            fn(pltpu.make_async_copy(stage.at[0, pl.ds(0, FFN_TM)], xs_hbm.at[pl.ds(start, FFN_TM)],
                                     pad_sem.at[0]))
            return c
        lax.fori_loop(nb_ref[0], xs_hbm.shape[0] // FFN_TM, one, 0)

    @pl.when(i == 1)
    def _():
        for_pad_rows(lambda cp: cp.wait())
        for_tail_blocks(lambda cp: cp.wait())

    x1 = x1_ref[...]
    ms = jnp.mean(x1 * x1, axis=-1, keepdims=True)
    stage[slot] = x1 * lax.rsqrt(ms + EPS) * g_ref[...]

    @pl.when(i == 0)
    def _():
        for_pad_rows(lambda cp: cp.start())
        for_tail_blocks(lambda cp: cp.start())

    def issue(r, carry):
        row_copy(slot, r, dest_ref[0, 0, r], sem.at[slot]).start(priority=0)
        row_copy(slot, r, dest_ref[0, 0, TOK_TM + r], sem.at[slot]).start(priority=1)
        return carry

    lax.fori_loop(0, TOK_TM, issue, 0, unroll=8)

    @pl.when(i > 0)
    def _():
        tile_wait(1 - slot)

    @pl.when(i == n - 1)
    def _():
        tile_wait(slot)


def _dispatch(x1, g, dest3, pad_row, pad_n, n_used, n_rows):
    nt = dest3.shape[0]
    assert nt >= 2 and TOK_TM >= FFN_TM
    return pl.pallas_call(
        _dispatch_kernel,
        grid_spec=pltpu.PrefetchScalarGridSpec(
            num_scalar_prefetch=3,
            grid=(nt,),
            in_specs=[
                pl.BlockSpec((1, 1, 2 * TOK_TM), lambda i, a, b, c: (i, 0, 0), memory_space=pltpu.SMEM),
                pl.BlockSpec((TOK_TM, D_MODEL), lambda i, a, b, c: (i, 0)),
                pl.BlockSpec((1, D_MODEL), lambda i, a, b, c: (0, 0)),
            ],
            out_specs=pl.BlockSpec(memory_space=pl.ANY),
            scratch_shapes=[pltpu.VMEM((2, TOK_TM, D_MODEL), F32),
                            pltpu.SemaphoreType.DMA((2,)), pltpu.SemaphoreType.DMA((1,))],
        ),
        out_shape=jax.ShapeDtypeStruct((n_rows, D_MODEL), F32),
        compiler_params=_cparams(("arbitrary",)),
        name="dispatch",
    )(pad_row, pad_n, n_used, dest3, x1, g.reshape(1, D_MODEL))


def _ffn_kernel(be_ref, nb_ref, nxt_ref, xs_ref, wg_hbm, wu_hbm, wd_hbm, y_ref,
                stg_g, stg_u, stg_d, wgb, wub, wdb, sem, *, layer):
    i = pl.program_id(0)
    nb = nb_ref[0]

    def weight_copies(e):
        return (pltpu.make_async_copy(wg_hbm.at[layer, e], stg_g, sem.at[0]),
                pltpu.make_async_copy(wu_hbm.at[layer, e], stg_u, sem.at[1]),
                pltpu.make_async_copy(wd_hbm.at[layer, e], stg_d, sem.at[2]))

    @pl.when(i == 0)
    def _():
        for cp in weight_copies(be_ref[0]):
            cp.start()

    @pl.when(i < nb)
    def _():
        e = be_ref[i]
        prev = be_ref[jnp.maximum(i - 1, 0)]

        @pl.when((i == 0) | (e != prev))
        def _():
            for cp in weight_copies(e):
                cp.wait()
            wgb[...] = stg_g[...].astype(BF16)
            wub[...] = stg_u[...].astype(BF16)
            wdb[...] = stg_d[...].astype(BF16)
            nxt = nxt_ref[e]

            @pl.when(nxt < N_EXPERTS)
            def _():
                for cp in weight_copies(nxt):
                    cp.start()

        xb = xs_ref[...].astype(BF16)
        g = jnp.dot(xb, wgb[...], preferred_element_type=F32)
        up = jnp.dot(xb, wub[...], preferred_element_type=F32)
        hid = (g * jax.nn.sigmoid(g) * up).astype(BF16)
        y_ref[...] = jnp.dot(hid, wdb[...], preferred_element_type=F32)

    @pl.when(i >= nb)
    def _():
        y_ref[...] = jnp.zeros(y_ref.shape, y_ref.dtype)


def _ffn(xs, blk_expert, n_used, next_used, w_gate, w_up, w_down, layer):
    n_rows = xs.shape[0]
    n_blocks = n_rows // FFN_TM
    hbm = pl.BlockSpec(memory_space=pl.ANY)
    return pl.pallas_call(
        functools.partial(_ffn_kernel, layer=layer),
        grid_spec=pltpu.PrefetchScalarGridSpec(
            num_scalar_prefetch=3,
            grid=(n_blocks,),
            in_specs=[
                pl.BlockSpec((FFN_TM, D_MODEL), lambda i, be, nb, nx: (jnp.minimum(i, nb[0] - 1), 0)),
                hbm, hbm, hbm,
            ],
            out_specs=pl.BlockSpec((FFN_TM, D_MODEL), lambda i, be, nb, nx: (i, 0)),
            scratch_shapes=[
                pltpu.VMEM((D_MODEL, D_EXPERT), F32),
                pltpu.VMEM((D_MODEL, D_EXPERT), F32),
                pltpu.VMEM((D_EXPERT, D_MODEL), F32),
                pltpu.VMEM((D_MODEL, D_EXPERT), BF16),
                pltpu.VMEM((D_MODEL, D_EXPERT), BF16),
                pltpu.VMEM((D_EXPERT, D_MODEL), BF16),
                pltpu.SemaphoreType.DMA((3,)),
            ],
        ),
        out_shape=jax.ShapeDtypeStruct((n_rows, D_MODEL), F32),
        compiler_params=_cparams(("arbitrary",)),
        name="ffn",
    )(blk_expert, n_used, next_used, xs, w_gate, w_up, w_down)


def _gather_rows(idx_ref, n, src_hbm, dst, sem):
    def body(k, carry):
        for parity in range(2):
            r = 2 * k + parity
            pltpu.make_async_copy(src_hbm.at[pl.ds(idx_ref[0, 0, r], 1)], dst.at[pl.ds(r, 1)],
                                  sem).start(priority=parity)
        return carry
    lax.fori_loop(0, n // 2, body, 0, unroll=4)


def _wait_rows(n, src_hbm, dst, sem):
    pltpu.make_async_copy(src_hbm.at[pl.ds(0, n)], dst, sem).wait()


def _combine_kernel(pos_ref, posn_ref, x1_ref, gt_ref, y_hbm, g_ref, o_ref, ybuf, sem, *, final_norm):
    i = pl.program_id(0)
    n = pl.num_programs(0)
    slot = lax.rem(i, 2)
    tm = TOK_TM

    @pl.when(i == 0)
    def _():
        _gather_rows(pos_ref, 2 * tm, y_hbm, ybuf.at[0], sem.at[0])

    @pl.when(i + 1 < n)
    def _():
        _gather_rows(posn_ref, 2 * tm, y_hbm, ybuf.at[1 - slot], sem.at[1 - slot])

    _wait_rows(2 * tm, y_hbm, ybuf.at[slot], sem.at[slot])
    out = x1_ref[...] + (ybuf[slot, 0:tm, :] * gt_ref[:, 0:1] + ybuf[slot, tm:2 * tm, :] * gt_ref[:, 1:2])
    if final_norm:
        ms = jnp.mean(out * out, axis=-1, keepdims=True)
        out = out * lax.rsqrt(ms + EPS) * g_ref[...]
    o_ref[...] = out


def _combine(x1, y, dest3, gates_t, g, final_norm):
    t = x1.shape[0]
    nt = t // TOK_TM
    return pl.pallas_call(
        functools.partial(_combine_kernel, final_norm=final_norm),
        grid=(nt,),
        in_specs=[
            pl.BlockSpec((1, 1, 2 * TOK_TM), lambda i: (i, 0, 0), memory_space=pltpu.SMEM),
            pl.BlockSpec((1, 1, 2 * TOK_TM), lambda i: (jnp.minimum(i + 1, nt - 1), 0, 0),
                         memory_space=pltpu.SMEM),
            pl.BlockSpec((TOK_TM, D_MODEL), lambda i: (i, 0)),
            pl.BlockSpec((TOK_TM, TOPK_IN_GROUP), lambda i: (i, 0)),
            pl.BlockSpec(memory_space=pl.ANY),
            pl.BlockSpec((1, D_MODEL), lambda i: (0, 0)),
        ],
        out_specs=pl.BlockSpec((TOK_TM, D_MODEL), lambda i: (i, 0)),
        out_shape=jax.ShapeDtypeStruct((t, D_MODEL), F32),
        scratch_shapes=[pltpu.VMEM((2, 2 * TOK_TM, D_MODEL), F32), pltpu.SemaphoreType.DMA((2,))],
        compiler_params=_cparams(("arbitrary",)),
        name="combine",
    )(dest3, dest3, x1, gates_t, y, g.reshape(1, D_MODEL))


def _row_tables(ri, counts_f, t):
    counts = counts_f[:, 0].astype(I32)
    padded = (counts + FFN_TM - 1) // FFN_TM * FFN_TM
    end_p = jnp.cumsum(padded)
    start_p = end_p - padded
    n_rows = t * TOPK_IN_GROUP + N_EXPERTS * FFN_TM
    blk_start = jnp.arange(n_rows // FFN_TM, dtype=I32) * FFN_TM
    blk_expert = jnp.minimum(jnp.sum((end_p[None, :] <= blk_start[:, None]).astype(I32), axis=1), N_EXPERTS - 1)
    n_used = (end_p[-1] // FFN_TM).astype(I32).reshape(1)
    ids = jnp.arange(N_EXPERTS, dtype=I32)
    later = (ids[None, :] > ids[:, None]) & (counts[None, :] > 0)
    next_used = jnp.min(jnp.where(later, ids[None, :], N_EXPERTS), axis=1).astype(I32)
    expert = ri[0:TOPK_IN_GROUP]
    rank = ri[TOPK_IN_GROUP:2 * TOPK_IN_GROUP]
    first = jnp.sum(jnp.where(expert[:, :, None] == jnp.arange(N_EXPERTS, dtype=I32), start_p, 0), axis=-1)
    dest = rank + first
    nt = t // TOK_TM
    dest3 = dest.reshape(TOPK_IN_GROUP, nt, TOK_TM).transpose(1, 0, 2).reshape(nt, 1, TOPK_IN_GROUP * TOK_TM)
    return (blk_expert, n_used, next_used, dest3, (start_p + counts).astype(I32), (padded - counts).astype(I32),
            n_rows)


def _router_weights(rg_w, rg_b, re_w, re_b):
    wrt = jnp.zeros((ROUTER_ROWS, D_MODEL), F32)
    wrt = wrt.at[0:N_GROUPS].set(rg_w.T).at[ROUTER_E0:ROUTER_E0 + N_EXPERTS].set(re_w.T)
    brt = jnp.zeros((ROUTER_ROWS,), F32)
    brt = brt.at[0:N_GROUPS].set(rg_b).at[ROUTER_E0:ROUTER_E0 + N_EXPERTS].set(re_b)
    return wrt.astype(BF16), brt.reshape(ROUTER_ROWS, 1)


def kernel(x, norm1_g, w_in, conv_a_w, conv_c_w, conv_c_b, ln_c_g, ln_c_b, w_out, norm2_g,
           router_g_w, router_g_b, router_e_w, router_e_b, w_gate, w_up, w_down, final_g):
    batch, seq, d = x.shape
    t = batch * seq
    depth = w_in.shape[0]
    xt = x.reshape(t, d)
    slopes = jnp.exp2(-8.0 * jnp.arange(1, N_HEADS + 1, dtype=F32) / N_HEADS)
    for l in range(depth):
        u = _in_proj(xt, norm1_g[l], w_in[l].astype(BF16))
        ya, yc = _mixers(u, conv_a_w[l], conv_c_w[l], conv_c_b[l], ln_c_g[l], ln_c_b[l], batch, seq)
        yb = _moba(u, slopes, batch, seq)
        wrt, brt = _router_weights(router_g_w[l], router_g_b[l], router_e_w[l], router_e_b[l])
        x1, ri, rf, counts = _out_proj(xt, ya, yb, yc, w_out[l].astype(BF16), norm2_g[l], wrt, brt)
        blk_expert, n_used, next_used, dest3, pad_row, pad_n, n_rows = _row_tables(ri, counts, t)
        xs = _dispatch(x1, norm2_g[l], dest3, pad_row, pad_n, n_used, n_rows)
        y = _ffn(xs, blk_expert, n_used, next_used, w_gate, w_up, w_down, l)
        xt = _combine(x1, y, dest3, rf[0:TOPK_IN_GROUP].T, final_g, final_norm=(l == depth - 1))
    return xt.reshape(batch, seq, d)
```

```python
import functools

import jax
import jax.numpy as jnp
from jax import lax
from jax.experimental import pallas as pl
from jax.experimental.pallas import tpu as pltpu

D_MODEL = 2048
D_CONV = 512
D_ATT = 1024
N_HEADS = 8
HEAD_DIM = 128
D_CFM = 512
D_MIX = D_CONV + D_ATT + D_CFM
D_IN = 3 * D_CONV + 3 * D_ATT + 2 * D_CFM
SHORT_CONV_W = 3
CFM_CONV_W = 31
MOBA_BLOCK = 256
MOBA_TOPK = 3
N_GROUPS = 4
EXPERTS_PER_GROUP = 8
N_EXPERTS = N_GROUPS * EXPERTS_PER_GROUP
TOPK_IN_GROUP = 2
D_EXPERT = 512
EPS = 1e-6
NEG = -1e30

LANES = 128
SUBLANES = 8
BF16_ROWS = 16
VMEM_LIMIT = 56 * 1024 * 1024

IN_TM = 512
IN_TN = D_IN // 2
MIX_TS = 512
HALO_A = 8
HALO_C = 32
MOBA_AUG = 2 * HEAD_DIM
MOBA_TILES_PER_STEP = 16
OUT_TM = 512
ROUTER_ROWS = LANES
ROUTER_E0 = SUBLANES
FFN_TM = 256
TOK_TM = 512

BF16 = jnp.bfloat16
F32 = jnp.float32
I32 = jnp.int32


def _cparams(sem):
    return pltpu.CompilerParams(dimension_semantics=sem, vmem_limit_bytes=VMEM_LIMIT)


def _in_proj_kernel(x_ref, g_ref, w_ref, o_ref, hn_ref):
    j = pl.program_id(1)

    @pl.when(j == 0)
    def _():
        x = x_ref[...]
        ms = jnp.mean(x * x, axis=-1, keepdims=True)
        hn_ref[...] = (x * lax.rsqrt(ms + EPS) * g_ref[...]).astype(BF16)

    col = j * IN_TN + lax.broadcasted_iota(I32, (1, IN_TN), 1)
    is_q = (col >= 3 * D_CONV) & (col < 3 * D_CONV + D_ATT)
    col_scale = jnp.where(is_q, HEAD_DIM ** -0.5, 1.0).astype(F32)
    acc = jnp.dot(hn_ref[...], w_ref[...], preferred_element_type=F32)
    o_ref[...] = (acc * col_scale).astype(o_ref.dtype)


def _in_proj(x, g, w_bf):
    t = x.shape[0]
    return pl.pallas_call(
        _in_proj_kernel,
        grid=(t // IN_TM, D_IN // IN_TN),
        in_specs=[
            pl.BlockSpec((IN_TM, D_MODEL), lambda i, j: (i, 0)),
            pl.BlockSpec((1, D_MODEL), lambda i, j: (0, 0)),
            pl.BlockSpec((D_MODEL, IN_TN), lambda i, j: (0, j)),
        ],
        out_specs=pl.BlockSpec((IN_TM, IN_TN), lambda i, j: (i, j)),
        out_shape=jax.ShapeDtypeStruct((t, D_IN), BF16),
        scratch_shapes=[pltpu.VMEM((IN_TM, D_MODEL), BF16)],
        compiler_params=_cparams(("arbitrary", "arbitrary")),
        name="in_proj",
    )(x, g.reshape(1, D_MODEL), w_bf)


def _mix_kernel(bg_ref, cg_ref, xv_ref, a_ref, gt_ref, wa_ref, wc_ref, bc_ref, lg_ref, lb_ref,
                ya_ref, yc_ref, bufa, bufc, shc):
    s = pl.program_id(1)
    ts = MIX_TS

    @pl.when(s == 0)
    def _():
        bufa[0:HALO_A, :] = jnp.zeros((HALO_A, D_CONV), F32)
        bufc[0:HALO_C, :] = jnp.zeros((HALO_C, D_CFM), F32)

    @pl.when(s > 0)
    def _():
        bufa[0:HALO_A, :] = bufa[ts:ts + HALO_A, :]
        bufc[0:HALO_C, :] = bufc[ts:ts + HALO_C, :]

    bufa[HALO_A:HALO_A + ts, :] = cg_ref[...].astype(F32) * xv_ref[...].astype(F32)
    acc = jnp.zeros((ts, D_CONV), F32)
    for j in range(SHORT_CONV_W):
        off = HALO_A - (SHORT_CONV_W - 1) + j
        acc = acc + wa_ref[j:j + 1, :] * bufa[off:off + ts, :]
    ya_ref[...] = (bg_ref[...].astype(F32) * acc).astype(ya_ref.dtype)

    a = a_ref[...].astype(F32)
    gate = gt_ref[...].astype(F32)
    bufc[HALO_C:HALO_C + ts, :] = a * jax.nn.sigmoid(gate)
    span = ts + HALO_C - SUBLANES
    for r in range(1, SUBLANES):
        shc[r - 1, :, :] = bufc[r:r + span, :]
    acc = jnp.zeros((ts, D_CFM), F32) + bc_ref[...]
    for j in range(CFM_CONV_W):
        off = HALO_C - (CFM_CONV_W - 1) + j
        a, r = divmod(off, SUBLANES)
        win = bufc[off:off + ts, :] if r == 0 else shc[r - 1, a * SUBLANES:a * SUBLANES + ts, :]
        acc = acc + wc_ref[j:j + 1, :] * win
    mu = jnp.mean(acc, axis=-1, keepdims=True)
    cen = acc - mu
    var = jnp.mean(cen * cen, axis=-1, keepdims=True)
    hn = cen * lax.rsqrt(var + EPS) * lg_ref[...] + lb_ref[...]
    yc_ref[...] = (hn * jax.nn.sigmoid(hn)).astype(yc_ref.dtype)


def _mixers(u, wa, wc, bc, lg, lb, batch, seq):
    t = u.shape[0]
    nst = seq // MIX_TS
    row = lambda b, s: b * nst + s
    ublk = lambda c: pl.BlockSpec((MIX_TS, D_CONV), lambda b, s, c=c: (row(b, s), c))
    small = lambda r: pl.BlockSpec((r, D_CONV), lambda b, s: (0, 0))
    c0 = (3 * D_CONV + 3 * D_ATT) // D_CFM
    return pl.pallas_call(
        _mix_kernel,
        grid=(batch, nst),
        in_specs=[ublk(0), ublk(1), ublk(2), ublk(c0), ublk(c0 + 1),
                  small(SHORT_CONV_W), small(CFM_CONV_W), small(1), small(1), small(1)],
        out_specs=[pl.BlockSpec((MIX_TS, D_CONV), lambda b, s: (row(b, s), 0)),
                   pl.BlockSpec((MIX_TS, D_CFM), lambda b, s: (row(b, s), 0))],
        out_shape=[jax.ShapeDtypeStruct((t, D_CONV), BF16), jax.ShapeDtypeStruct((t, D_CFM), BF16)],
        scratch_shapes=[pltpu.VMEM((HALO_A + MIX_TS, D_CONV), F32),
                        pltpu.VMEM((HALO_C + MIX_TS, D_CFM), F32),
                        pltpu.VMEM((SUBLANES - 1, MIX_TS + HALO_C - SUBLANES, D_CFM), F32)],
        compiler_params=_cparams(("arbitrary", "arbitrary")),
        name="mixers",
    )(u, u, u, u, u, wa, wc, bc.reshape(1, D_CFM), lg.reshape(1, D_CFM), lb.reshape(1, D_CFM))


_NT = (((1,), (1,)), ((), ()))

F_MASK = HEAD_DIM
F_BIAS = HEAD_DIM + BF16_ROWS


def _moba_prepare(slope, q_ref, k_ref, v_ref, kaug, vt, qaug, ident):
    blk = MOBA_BLOCK
    seq = k_ref.shape[0]
    nb = seq // blk
    ri = lax.broadcasted_iota(I32, (blk, blk), 0)
    ci = lax.broadcasted_iota(I32, (blk, blk), 1)
    ident[...] = jnp.where(ri == ci, 1.0, 0.0).astype(BF16)
    eye_hd = ident[0:HEAD_DIM, 0:HEAD_DIM]

    kf = k_ref[...].astype(F32).reshape(nb, blk, HEAD_DIM)
    km = jnp.sum(kf, axis=1) * (1.0 / blk)
    km_hi = km.astype(BF16)
    km_lo = (km - km_hi.astype(F32)).astype(BF16)

    kaug[:, 0:HEAD_DIM] = k_ref[...]
    lane = lax.broadcasted_iota(I32, (blk, LANES), 1)
    cc = lax.broadcasted_iota(I32, (blk, LANES), 0).astype(F32)
    fb = F_BIAS - HEAD_DIM

    common = jnp.where((lane == fb) | (lane == fb + 1), 1.0, jnp.where(lane == fb + 2, slope * cc, 0.0))
    for j in range(nb):
        feat = jnp.where(lane == j, 1.0, jnp.where(lane == fb + 3, slope * float(j * blk), common))
        kaug[j * blk:(j + 1) * blk, HEAD_DIM:MOBA_AUG] = feat.astype(BF16)
        vt[j] = lax.dot_general(eye_hd, v_ref[j * blk:(j + 1) * blk, :], _NT,
                                preferred_element_type=F32).astype(BF16)

    q = q_ref[...]
    qaug[0:HEAD_DIM, :] = lax.dot_general(eye_hd, q, _NT, preferred_element_type=F32).astype(BF16)

    gate = (lax.dot_general(km_hi, q, _NT, preferred_element_type=F32)
            + lax.dot_general(km_lo, q, _NT, preferred_element_type=F32))
    sub = lax.broadcasted_iota(I32, (nb, seq), 0)
    tpos = lax.broadcasted_iota(I32, (nb, seq), 1)
    qblk = lax.shift_right_logical(tpos, blk.bit_length() - 1)
    rank = jnp.zeros((nb, seq), F32)
    for jp in range(nb):
        row = gate[jp:jp + 1, :]
        beats = (row > gate) | ((row == gate) & (sub > jp))
        rank = rank + jnp.where(beats & (qblk > jp), 1.0, 0.0)
    allowed = ((sub < qblk) & (rank < MOBA_TOPK)) | (sub == qblk)
    qaug[F_MASK:F_MASK + BF16_ROWS, :] = jnp.where(allowed, 0.0, NEG).astype(BF16)

    rq = (tpos & (blk - 1)).astype(F32)
    bias = jnp.where(sub == 0, -slope * rq, 0.0)
    bias = jnp.where(sub == 1, -slope * (qblk * blk).astype(F32), bias)
    bias = jnp.where((sub == 2) | (sub == 3), 1.0, bias)
    qaug[F_BIAS:F_BIAS + BF16_ROWS, :] = bias.astype(BF16)
    qaug[F_BIAS + BF16_ROWS:MOBA_AUG, :] = jnp.zeros((MOBA_AUG - F_BIAS - BF16_ROWS, seq), BF16)


def _moba_tile(n, kaug, vt, qaug, ident, o_ref):
    blk = MOBA_BLOCK
    qa = qaug[:, n * blk:(n + 1) * blk]
    kc = lax.broadcasted_iota(I32, (blk, blk), 0)
    qr = lax.broadcasted_iota(I32, (blk, blk), 1)
    ss = [jnp.dot(kaug[j * blk:(j + 1) * blk, :], qa, preferred_element_type=F32) for j in range(n + 1)]
    ss[n] = jnp.where(kc <= qr, ss[n], NEG)
    m = jnp.max(ss[0], axis=0, keepdims=True)
    for j in range(1, n + 1):
        m = jnp.maximum(m, jnp.max(ss[j], axis=0, keepdims=True))
    l = jnp.zeros((1, blk), F32)
    acc = jnp.zeros((HEAD_DIM, blk), F32)
    for j in range(n + 1):
        p = jnp.exp(ss[j] - m)
        l = l + jnp.sum(p, axis=0, keepdims=True)
        acc = acc + jnp.dot(vt[j], p.astype(BF16), preferred_element_type=F32)
    out_t = (acc / l).astype(BF16)
    o_ref[n * blk:(n + 1) * blk, :] = lax.dot_general(ident[...], out_t, _NT,
                                                      preferred_element_type=F32).astype(o_ref.dtype)


def _moba_scores(n, kaug, qaug):
    blk = MOBA_BLOCK
    qa = qaug[:, n * blk:(n + 1) * blk]
    kc = lax.broadcasted_iota(I32, (blk, blk), 0)
    qr = lax.broadcasted_iota(I32, (blk, blk), 1)
    ss = [jnp.dot(kaug[j * blk:(j + 1) * blk, :], qa, preferred_element_type=F32) for j in range(n + 1)]
    ss[n] = jnp.where(kc <= qr, ss[n], NEG)
    m = jnp.max(ss[0], axis=0, keepdims=True)
    for j in range(1, n + 1):
        m = jnp.maximum(m, jnp.max(ss[j], axis=0, keepdims=True))
    return ss, m


def _moba_finish(n, scored, vt, ident, o_ref):
    blk = MOBA_BLOCK
    ss, m = scored
    l = jnp.zeros((1, blk), F32)
    acc = jnp.zeros((HEAD_DIM, blk), F32)
    for j in range(n + 1):
        p = jnp.exp(ss[j] - m)
        l = l + jnp.sum(p, axis=0, keepdims=True)
        acc = acc + jnp.dot(vt[j], p.astype(BF16), preferred_element_type=F32)
    out_t = (acc / l).astype(BF16)
    o_ref[n * blk:(n + 1) * blk, :] = lax.dot_general(ident[...], out_t, _NT,
                                                      preferred_element_type=F32).astype(o_ref.dtype)


def _moba_kernel(slopes_ref, q_ref, k_ref, v_ref, o_ref, kaug, vt, qaug, ident):
    step = pl.program_id(2)
    nq = k_ref.shape[0] // MOBA_BLOCK

    @pl.when(step == 0)
    def _():
        _moba_prepare(slopes_ref[pl.program_id(1)], q_ref, k_ref, v_ref, kaug, vt, qaug, ident)

    for s in range(nq // MOBA_TILES_PER_STEP):
        @pl.when(step == s)
        def _(s=s):
            tiles = _moba_step_tiles(s, nq)
            pending = _moba_scores(tiles[0], kaug, qaug)
            for idx, n in enumerate(tiles):
                nxt = _moba_scores(tiles[idx + 1], kaug, qaug) if idx + 1 < len(tiles) else None
                _moba_finish(n, pending, vt, ident, o_ref)
                pending = nxt


def _moba_step_tiles(s, nq):
    pairs = MOBA_TILES_PER_STEP // 2
    out = []
    for k in range(s * pairs, (s + 1) * pairs):
        out += [k, nq - 1 - k]
    return out


def _moba(u, slopes, batch, seq):
    t = u.shape[0]
    nq = seq // MOBA_BLOCK
    assert nq == BF16_ROWS
    qc = (3 * D_CONV) // HEAD_DIM
    kc = qc + N_HEADS
    vc = kc + N_HEADS
    whole = lambda c0: pl.BlockSpec((seq, HEAD_DIM), lambda b, h, i, s: (b, c0 + h))
    return pl.pallas_call(
        _moba_kernel,
        grid_spec=pltpu.PrefetchScalarGridSpec(
            num_scalar_prefetch=1,
            grid=(batch, N_HEADS, nq // MOBA_TILES_PER_STEP),
            in_specs=[whole(qc), whole(kc), whole(vc)],
            out_specs=pl.BlockSpec((seq, HEAD_DIM), lambda b, h, i, s: (b, h)),
            scratch_shapes=[
                pltpu.VMEM((seq, MOBA_AUG), BF16),
                pltpu.VMEM((nq, HEAD_DIM, MOBA_BLOCK), BF16),
                pltpu.VMEM((MOBA_AUG, seq), BF16),
                pltpu.VMEM((MOBA_BLOCK, MOBA_BLOCK), BF16),
            ],
        ),
        out_shape=jax.ShapeDtypeStruct((t, D_ATT), BF16),
        compiler_params=_cparams(("arbitrary", "arbitrary", "arbitrary")),
        name="moba",
    )(slopes, u, u, u)


def _first_max(v, rows):
    top = jnp.max(v, axis=0, keepdims=True)
    idx = jnp.min(jnp.where(v == top, rows, float(v.shape[0])), axis=0, keepdims=True)
    return top, idx


def _router(lg, tri_ref, carry_ref):
    tm = lg.shape[1]
    rows = lax.broadcasted_iota(I32, (SUBLANES, tm), 0).astype(F32)
    grp_lg = jnp.where(rows < N_GROUPS, lg[0:SUBLANES, :], NEG)
    ex = jnp.exp(grp_lg - jnp.max(grp_lg, axis=0, keepdims=True))
    pg = ex / jnp.sum(ex, axis=0, keepdims=True)
    p_grp, grp = _first_max(pg, rows)
    le = jnp.zeros((EXPERTS_PER_GROUP, tm), F32)
    for g in range(N_GROUPS):
        r0 = ROUTER_E0 + g * EXPERTS_PER_GROUP
        le = jnp.where(grp == g, lg[r0:r0 + EXPERTS_PER_GROUP, :], le)
    ex = jnp.exp(le - jnp.max(le, axis=0, keepdims=True))
    q = ex / jnp.sum(ex, axis=0, keepdims=True)
    q1, i1 = _first_max(q, rows)
    q2, i2 = _first_max(jnp.where(rows == i1, -1.0, q), rows)
    den = q1 + q2
    g0 = p_grp * q1 / den
    g1 = p_grp * q2 / den
    e0 = grp * EXPERTS_PER_GROUP + i1
    e1 = grp * EXPERTS_PER_GROUP + i2

    erow = lax.broadcasted_iota(I32, (N_EXPERTS, tm), 0).astype(F32)
    is0 = erow == e0
    is1 = erow == e1
    onehot = jnp.where(is0 | is1, 1.0, 0.0)
    before = jnp.dot(onehot.astype(BF16), tri_ref[...], preferred_element_type=F32) + carry_ref[:, 0:1]
    r0 = jnp.sum(jnp.where(is0, before, 0.0), axis=0, keepdims=True)
    r1 = jnp.sum(jnp.where(is1, before, 0.0), axis=0, keepdims=True)
    carry_ref[...] = carry_ref[...] + jnp.sum(onehot, axis=1, keepdims=True)
    return e0, e1, g0, g1, r0, r1


def _out_proj_kernel(x_ref, ya_ref, yb_ref, yc_ref, w_ref, g_ref, wrt_ref, brt_ref,
                     x1_ref, h2_ref, ri_ref, rf_ref, cnt_ref, tri_ref, carry_ref):
    tm = OUT_TM

    @pl.when(pl.program_id(0) == 0)
    def _():
        a = lax.broadcasted_iota(I32, (tm, tm), 0)
        b = lax.broadcasted_iota(I32, (tm, tm), 1)
        tri_ref[...] = jnp.where(a < b, 1.0, 0.0).astype(BF16)
        carry_ref[...] = jnp.zeros(carry_ref.shape, F32)

    acc = x_ref[...]
    acc = acc + jnp.dot(ya_ref[...], w_ref[0:D_CONV, :], preferred_element_type=F32)
    acc = acc + jnp.dot(yb_ref[...], w_ref[D_CONV:D_CONV + D_ATT, :], preferred_element_type=F32)
    acc = acc + jnp.dot(yc_ref[...], w_ref[D_CONV + D_ATT:D_MIX, :], preferred_element_type=F32)
    x1_ref[...] = acc
    ms = jnp.mean(acc * acc, axis=-1, keepdims=True)
    h2 = acc * lax.rsqrt(ms + EPS) * g_ref[...]
    h2_ref[...] = h2

    lg = lax.dot_general(wrt_ref[...], h2.astype(BF16), _NT, preferred_element_type=F32) + brt_ref[...]
    e0, e1, g0, g1, r0, r1 = _router(lg, tri_ref, carry_ref)
    row = lax.broadcasted_iota(I32, (SUBLANES, tm), 0)
    ints = jnp.where(row == 0, e0, jnp.where(row == 1, e1, jnp.where(row == 2, r0, jnp.where(row == 3, r1, 0.0))))
    ri_ref[...] = ints.astype(I32)
    rf_ref[...] = jnp.where(row == 0, g0, jnp.where(row == 1, g1, 0.0))
    cnt_ref[...] = carry_ref[...]


def _out_proj(x, ya, yb, yc, w_bf, g, wrt_bf, brt):
    t = x.shape[0]
    rows = lambda n: pl.BlockSpec((OUT_TM, n), lambda i: (i, 0))
    whole = lambda a, b: pl.BlockSpec((a, b), lambda i: (0, 0))
    cols = pl.BlockSpec((SUBLANES, OUT_TM), lambda i: (0, i))
    return pl.pallas_call(
        _out_proj_kernel,
        grid=(t // OUT_TM,),
        in_specs=[rows(D_MODEL), rows(D_CONV), rows(D_ATT), rows(D_CFM),
                  pl.BlockSpec((D_MIX, D_MODEL), lambda i: (0, 0), pipeline_mode=pl.Buffered(1)),
                  whole(1, D_MODEL), whole(ROUTER_ROWS, D_MODEL), whole(ROUTER_ROWS, 1)],
        out_specs=[rows(D_MODEL), rows(D_MODEL), cols, cols, whole(N_EXPERTS, LANES)],
        out_shape=[jax.ShapeDtypeStruct((t, D_MODEL), F32), jax.ShapeDtypeStruct((t, D_MODEL), F32),
                   jax.ShapeDtypeStruct((SUBLANES, t), I32), jax.ShapeDtypeStruct((SUBLANES, t), F32),
                   jax.ShapeDtypeStruct((N_EXPERTS, LANES), F32)],
        scratch_shapes=[pltpu.VMEM((OUT_TM, OUT_TM), BF16), pltpu.VMEM((N_EXPERTS, LANES), F32)],
        compiler_params=_cparams(("arbitrary",)),
        name="out_proj",
    )(x, ya, yb, yc, w_bf, g.reshape(1, D_MODEL), wrt_bf, brt)


def _dispatch_kernel(pad_row_ref, pad_n_ref, nb_ref, dest_ref, h2_ref, xs_hbm, stage, sem, pad_sem):
    i = pl.program_id(0)
    n = pl.num_programs(0)
    slot = lax.rem(i, 2)

    def row_copy(s_idx, r, dst_row, s):
        return pltpu.make_async_copy(stage.at[s_idx, pl.ds(r, 1)], xs_hbm.at[pl.ds(dst_row, 1)], s)

    def tile_wait(s_idx):
        for _ in range(TOPK_IN_GROUP):
            pltpu.make_async_copy(stage.at[s_idx], xs_hbm.at[pl.ds(0, TOK_TM)], sem.at[s_idx]).wait()

    def for_pad_rows(fn):
        def per_expert(e, carry):
            def one(k, c):
                fn(row_copy(0, 0, pad_row_ref[e] + k, pad_sem.at[0]))
                return c
            lax.fori_loop(0, pad_n_ref[e], one, 0)
            return carry
        lax.fori_loop(0, N_EXPERTS, per_expert, 0)

    def for_tail_blocks(fn):
        def one(b, c):
            start = pl.multiple_of(b * FFN_TM, FFN_TM)
            fn(pltpu.make_async_copy(stage.at[0, pl.ds(0, FFN_TM)], xs_hbm.at[pl.ds(start, FFN_TM)],
                                     pad_sem.at[0]))
            return c
        lax.fori_loop(nb_ref[0], xs_hbm.shape[0] // FFN_TM, one, 0)

    @pl.when(i == 1)
    def _():
        for_pad_rows(lambda cp: cp.wait())
        for_tail_blocks(lambda cp: cp.wait())

    stage[slot] = h2_ref[...]

    @pl.when(i == 0)
    def _():
        for_pad_rows(lambda cp: cp.start())
        for_tail_blocks(lambda cp: cp.start())

    def issue(r, carry):
        row_copy(slot, r, dest_ref[0, 0, r], sem.at[slot]).start(priority=0)
        row_copy(slot, r, dest_ref[0, 0, TOK_TM + r], sem.at[slot]).start(priority=1)
        return carry

    lax.fori_loop(0, TOK_TM, issue, 0, unroll=8)

    @pl.when(i > 0)
    def _():
        tile_wait(1 - slot)

    @pl.when(i == n - 1)
    def _():
        tile_wait(slot)


def _dispatch(h2, dest3, pad_row, pad_n, n_used, n_rows):
    nt = dest3.shape[0]
    assert nt >= 2 and TOK_TM >= FFN_TM
    return pl.pallas_call(
        _dispatch_kernel,
        grid_spec=pltpu.PrefetchScalarGridSpec(
            num_scalar_prefetch=3,
            grid=(nt,),
            in_specs=[
                pl.BlockSpec((1, 1, 2 * TOK_TM), lambda i, a, b, c: (i, 0, 0), memory_space=pltpu.SMEM),
                pl.BlockSpec((TOK_TM, D_MODEL), lambda i, a, b, c: (i, 0)),
            ],
            out_specs=pl.BlockSpec(memory_space=pl.ANY),
            scratch_shapes=[pltpu.VMEM((2, TOK_TM, D_MODEL), F32),
                            pltpu.SemaphoreType.DMA((2,)), pltpu.SemaphoreType.DMA((1,))],
        ),
        out_shape=jax.ShapeDtypeStruct((n_rows, D_MODEL), F32),
        compiler_params=_cparams(("arbitrary",)),
        name="dispatch",
    )(pad_row, pad_n, n_used, dest3, h2)


def _ffn_kernel(be_ref, nb_ref, nxt_ref, xs_ref, wg_hbm, wu_hbm, wd_hbm, y_ref,
                stg_g, stg_u, stg_d, wgb, wub, wdb, sem, *, layer):
    i = pl.program_id(0)
    nb = nb_ref[0]

    def weight_copies(e):
        return (pltpu.make_async_copy(wg_hbm.at[layer, e], stg_g, sem.at[0]),
                pltpu.make_async_copy(wu_hbm.at[layer, e], stg_u, sem.at[1]),
                pltpu.make_async_copy(wd_hbm.at[layer, e], stg_d, sem.at[2]))

    @pl.when(i == 0)
    def _():
        for cp in weight_copies(be_ref[0]):
            cp.start()

    @pl.when(i < nb)
    def _():
        e = be_ref[i]
        prev = be_ref[jnp.maximum(i - 1, 0)]

        @pl.when((i == 0) | (e != prev))
        def _():
            for cp in weight_copies(e):
                cp.wait()
            wgb[...] = stg_g[...].astype(BF16)
            wub[...] = stg_u[...].astype(BF16)
            wdb[...] = stg_d[...].astype(BF16)
            nxt = nxt_ref[e]

            @pl.when(nxt < N_EXPERTS)
            def _():
                for cp in weight_copies(nxt):
                    cp.start()

        xb = xs_ref[...].astype(BF16)
        g = jnp.dot(xb, wgb[...], preferred_element_type=F32)
        up = jnp.dot(xb, wub[...], preferred_element_type=F32)
        hid = (g * jax.nn.sigmoid(g) * up).astype(BF16)
        y_ref[...] = jnp.dot(hid, wdb[...], preferred_element_type=F32)

    @pl.when(i >= nb)
    def _():
        y_ref[...] = jnp.zeros(y_ref.shape, y_ref.dtype)


def _ffn(xs, blk_expert, n_used, next_used, w_gate, w_up, w_down, layer):
    n_rows = xs.shape[0]
    n_blocks = n_rows // FFN_TM
    hbm = pl.BlockSpec(memory_space=pl.ANY)
    return pl.pallas_call(
        functools.partial(_ffn_kernel, layer=layer),
        grid_spec=pltpu.PrefetchScalarGridSpec(
            num_scalar_prefetch=3,
            grid=(n_blocks,),
            in_specs=[
                pl.BlockSpec((FFN_TM, D_MODEL), lambda i, be, nb, nx: (jnp.minimum(i, nb[0] - 1), 0)),
                hbm, hbm, hbm,
            ],
            out_specs=pl.BlockSpec((FFN_TM, D_MODEL), lambda i, be, nb, nx: (i, 0)),
            scratch_shapes=[
                pltpu.VMEM((D_MODEL, D_EXPERT), F32),
                pltpu.VMEM((D_MODEL, D_EXPERT), F32),
                pltpu.VMEM((D_EXPERT, D_MODEL), F32),
                pltpu.VMEM((D_MODEL, D_EXPERT), BF16),
                pltpu.VMEM((D_MODEL, D_EXPERT), BF16),
                pltpu.VMEM((D_EXPERT, D_MODEL), BF16),
                pltpu.SemaphoreType.DMA((3,)),
            ],
        ),
        out_shape=jax.ShapeDtypeStruct((n_rows, D_MODEL), F32),
        compiler_params=_cparams(("arbitrary",)),
        name="ffn",
    )(blk_expert, n_used, next_used, xs, w_gate, w_up, w_down)


def _gather_rows(idx_ref, n, src_hbm, dst, sem):
    def body(k, carry):
        for parity in range(2):
            r = 2 * k + parity
            pltpu.make_async_copy(src_hbm.at[pl.ds(idx_ref[0, 0, r], 1)], dst.at[pl.ds(r, 1)],
                                  sem).start(priority=parity)
        return carry
    lax.fori_loop(0, n // 2, body, 0, unroll=4)


def _wait_rows(n, src_hbm, dst, sem):
    pltpu.make_async_copy(src_hbm.at[pl.ds(0, n)], dst, sem).wait()


def _combine_kernel(pos_ref, posn_ref, x1_ref, gt_ref, y_hbm, g_ref, o_ref, ybuf, sem, *, final_norm):
    i = pl.program_id(0)
    n = pl.num_programs(0)
    slot = lax.rem(i, 2)
    tm = TOK_TM

    @pl.when(i == 0)
    def _():
        _gather_rows(pos_ref, 2 * tm, y_hbm, ybuf.at[0], sem.at[0])

    @pl.when(i + 1 < n)
    def _():
        _gather_rows(posn_ref, 2 * tm, y_hbm, ybuf.at[1 - slot], sem.at[1 - slot])

    _wait_rows(2 * tm, y_hbm, ybuf.at[slot], sem.at[slot])
    out = x1_ref[...] + (ybuf[slot, 0:tm, :] * gt_ref[:, 0:1] + ybuf[slot, tm:2 * tm, :] * gt_ref[:, 1:2])
    if final_norm:
        ms = jnp.mean(out * out, axis=-1, keepdims=True)
        out = out * lax.rsqrt(ms + EPS) * g_ref[...]
    o_ref[...] = out


def _combine(x1, y, dest3, gates_t, g, final_norm):
    t = x1.shape[0]
    nt = t // TOK_TM
    return pl.pallas_call(
        functools.partial(_combine_kernel, final_norm=final_norm),
        grid=(nt,),
        in_specs=[
            pl.BlockSpec((1, 1, 2 * TOK_TM), lambda i: (i, 0, 0), memory_space=pltpu.SMEM),
            pl.BlockSpec((1, 1, 2 * TOK_TM), lambda i: (jnp.minimum(i + 1, nt - 1), 0, 0),
                         memory_space=pltpu.SMEM),
            pl.BlockSpec((TOK_TM, D_MODEL), lambda i: (i, 0)),
            pl.BlockSpec((TOK_TM, TOPK_IN_GROUP), lambda i: (i, 0)),
            pl.BlockSpec(memory_space=pl.ANY),
            pl.BlockSpec((1, D_MODEL), lambda i: (0, 0)),
        ],
        out_specs=pl.BlockSpec((TOK_TM, D_MODEL), lambda i: (i, 0)),
        out_shape=jax.ShapeDtypeStruct((t, D_MODEL), F32),
        scratch_shapes=[pltpu.VMEM((2, 2 * TOK_TM, D_MODEL), F32), pltpu.SemaphoreType.DMA((2,))],
        compiler_params=_cparams(("arbitrary",)),
        name="combine",
    )(dest3, dest3, x1, gates_t, y, g.reshape(1, D_MODEL))


def _row_tables(ri, counts_f, t):
    counts = counts_f[:, 0].astype(I32)
    padded = (counts + FFN_TM - 1) // FFN_TM * FFN_TM
    end_p = jnp.cumsum(padded)
    start_p = end_p - padded
    n_rows = t * TOPK_IN_GROUP + N_EXPERTS * FFN_TM
    blk_start = jnp.arange(n_rows // FFN_TM, dtype=I32) * FFN_TM
    blk_expert = jnp.minimum(jnp.sum((end_p[None, :] <= blk_start[:, None]).astype(I32), axis=1), N_EXPERTS - 1)
    n_used = (end_p[-1] // FFN_TM).astype(I32).reshape(1)
    ids = jnp.arange(N_EXPERTS, dtype=I32)
    later = (ids[None, :] > ids[:, None]) & (counts[None, :] > 0)
    next_used = jnp.min(jnp.where(later, ids[None, :], N_EXPERTS), axis=1).astype(I32)
    expert = ri[0:TOPK_IN_GROUP]
    rank = ri[TOPK_IN_GROUP:2 * TOPK_IN_GROUP]
    first = jnp.sum(jnp.where(expert[:, :, None] == jnp.arange(N_EXPERTS, dtype=I32), start_p, 0), axis=-1)
    dest = rank + first
    nt = t // TOK_TM
    dest3 = dest.reshape(TOPK_IN_GROUP, nt, TOK_TM).transpose(1, 0, 2).reshape(nt, 1, TOPK_IN_GROUP * TOK_TM)
    return (blk_expert, n_used, next_used, dest3, (start_p + counts).astype(I32), (padded - counts).astype(I32),
            n_rows)


def _router_weights(rg_w, rg_b, re_w, re_b):
    wrt = jnp.zeros((ROUTER_ROWS, D_MODEL), F32)
    wrt = wrt.at[0:N_GROUPS].set(rg_w.T).at[ROUTER_E0:ROUTER_E0 + N_EXPERTS].set(re_w.T)
    brt = jnp.zeros((ROUTER_ROWS,), F32)
    brt = brt.at[0:N_GROUPS].set(rg_b).at[ROUTER_E0:ROUTER_E0 + N_EXPERTS].set(re_b)
    return wrt.astype(BF16), brt.reshape(ROUTER_ROWS, 1)


def kernel(x, norm1_g, w_in, conv_a_w, conv_c_w, conv_c_b, ln_c_g, ln_c_b, w_out, norm2_g,
           router_g_w, router_g_b, router_e_w, router_e_b, w_gate, w_up, w_down, final_g):
    batch, seq, d = x.shape
    t = batch * seq
    depth = w_in.shape[0]
    xt = x.reshape(t, d)
    slopes = jnp.exp2(-8.0 * jnp.arange(1, N_HEADS + 1, dtype=F32) / N_HEADS)
    for l in range(depth):
        u = _in_proj(xt, norm1_g[l], w_in[l].astype(BF16))
        ya, yc = _mixers(u, conv_a_w[l], conv_c_w[l], conv_c_b[l], ln_c_g[l], ln_c_b[l], batch, seq)
        yb = _moba(u, slopes, batch, seq)
        wrt, brt = _router_weights(router_g_w[l], router_g_b[l], router_e_w[l], router_e_b[l])
        x1, h2, ri, rf, counts = _out_proj(xt, ya, yb, yc, w_out[l].astype(BF16), norm2_g[l], wrt, brt)
        blk_expert, n_used, next_used, dest3, pad_row, pad_n, n_rows = _row_tables(ri, counts, t)
        xs = _dispatch(h2, dest3, pad_row, pad_n, n_used, n_rows)
        y = _ffn(xs, blk_expert, n_used, next_used, w_gate, w_up, w_down, l)
        xt = _combine(x1, y, dest3, rf[0:TOPK_IN_GROUP].T, final_g, final_norm=(l == depth - 1))
    return xt.reshape(batch, seq, d)
```
